```python
import jax, jax.numpy as jnp
from jax import lax
import numpy as np

D_MODEL = 1024
BATCH = 32
SEQ = 256
DEPTH = 2
DEC_BATCH = 4
DEC_SEQ = 4096
PAST_LEN = 512

GRID_W = 64
POOL_WIDTH = 256
POOL_GROUPS = 4
POOL_WINDOWS = (2, 4, 8, 16)
POOL_GROUP_DIM = POOL_WIDTH // POOL_GROUPS
CONV_WIDTH = 256
CONV_TAPS = 31
N_HEADS = 8
N_KV_HEADS = 2
HEAD_DIM = 64
Q_GROUP = N_HEADS // N_KV_HEADS
ATTN_WIDTH = N_HEADS * HEAD_DIM
KV_WIDTH = N_KV_HEADS * HEAD_DIM
MIX_WIDTH = POOL_WIDTH + CONV_WIDTH + ATTN_WIDTH
ATTN_OFFSET = POOL_WIDTH + 2 * CONV_WIDTH
IN_WIDTH = ATTN_OFFSET + ATTN_WIDTH + 2 * KV_WIDTH
WINDOW = 128
BLOCK = 128
D_FF = 2816
N_MOD = 9
ROPE_THETA = 10000.0
EPS = 1e-6
NEG_INF = -1e30

kernel_name = 'hymba_pool_conv_swa_macaron_dit_step'


def _rms(x, g):
    xf = x.astype(jnp.float32)
    y = xf * lax.rsqrt(jnp.mean(xf * xf, axis=-1, keepdims=True) + EPS)
    return (y * g.astype(jnp.float32)).astype(x.dtype)


def _swiglu(h, wi, wo):
    gate, up = jnp.split(h @ wi, 2, axis=-1)
    return (jax.nn.silu(gate) * up) @ wo


def _modulations(cond, w, b):
    m = jax.nn.silu(cond) @ w + b
    return jnp.split(m[:, None, :], N_MOD, axis=-1)


def _pool_mixer(u, pool_w, pool_scale):
    n = u.shape[1]
    uf = u.astype(jnp.float32)
    cs = jnp.pad(jnp.cumsum(uf, axis=1), ((0, 0), (1, 0), (0, 0)))
    t = jnp.arange(n)
    outs = []
    for j, w in enumerate(POOL_WINDOWS):
        lo = jnp.clip(t - w // 2, 0, n)
        hi = jnp.clip(t + w // 2, 0, n)
        sl = slice(j * POOL_GROUP_DIM, (j + 1) * POOL_GROUP_DIM)
        csj = cs[..., sl]
        mean = (csj[:, hi] - csj[:, lo]) / (hi - lo).astype(jnp.float32)[None, :, None]
        pooled = (mean - uf[..., sl]).astype(u.dtype)
        outs.append(pooled @ pool_w[j])
    return jnp.concatenate(outs, axis=-1) * pool_scale


def _conv_mixer(g, dw, db, norm_g, pw):
    glu = g[..., :CONV_WIDTH] * jax.nn.sigmoid(g[..., CONV_WIDTH:])
    y = lax.conv_general_dilated(
        glu, dw[:, None, :].astype(glu.dtype), window_strides=(1,),
        padding=[(CONV_TAPS // 2, CONV_TAPS // 2)],
        dimension_numbers=('NWC', 'WIO', 'NWC'), feature_group_count=CONV_WIDTH) + db
    return jax.nn.silu(_rms(y, norm_g)) @ pw


def _qkv(u, q_g, k_g):
    b, n, _ = u.shape
    q = u[..., :ATTN_WIDTH].reshape(b, n, N_HEADS, HEAD_DIM)
    k = u[..., ATTN_WIDTH:ATTN_WIDTH + KV_WIDTH].reshape(b, n, N_KV_HEADS, HEAD_DIM)
    v = u[..., ATTN_WIDTH + KV_WIDTH:].reshape(b, n, N_KV_HEADS, HEAD_DIM)
    return _rms(q, q_g), _rms(k, k_g), v


def _rope_2d(x, n):
    rows = n // GRID_W
    row = jnp.repeat(jnp.arange(rows), GRID_W).astype(jnp.float32)
    col = jnp.tile(jnp.arange(GRID_W), rows).astype(jnp.float32)
    half = HEAD_DIM // 2
    inv = ROPE_THETA ** (-jnp.arange(0, half, 2, dtype=jnp.float32) / half)

    def rot(xa, pos):
        ang = pos[:, None] * inv[None, :]
        cos = jnp.cos(ang)[None, :, None, :]
        sin = jnp.sin(ang)[None, :, None, :]
        x1, x2 = xa[..., :half // 2], xa[..., half // 2:]
        return jnp.concatenate([x1 * cos - x2 * sin, x2 * cos + x1 * sin], axis=-1)

    xf = x.astype(jnp.float32)
    return jnp.concatenate([rot(xf[..., :half], row), rot(xf[..., half:], col)], axis=-1).astype(x.dtype)


def _softmax_with_sink(s, sink):
    b, _, _, nq, _ = s.shape
    sk = jnp.broadcast_to(sink.astype(jnp.float32).reshape(N_KV_HEADS, Q_GROUP)[None, :, :, None, None],
                          (b, N_KV_HEADS, Q_GROUP, nq, 1))
    p = jax.nn.softmax(jnp.concatenate([s, sk], axis=-1), axis=-1)
    return p[..., :-1]


def _context_attention(q, k, v, sink):
    b, n = q.shape[:2]
    nb = n // BLOCK
    qb = jnp.moveaxis(q.reshape(b, nb, BLOCK, N_KV_HEADS, Q_GROUP, HEAD_DIM), 1, 0)
    scale = HEAD_DIM ** -0.5

    def one(qi):
        s = jnp.einsum('bqkgd,bskd->bkgqs', qi, k).astype(jnp.float32) * scale
        p = _softmax_with_sink(s, sink).astype(v.dtype)
        return jnp.einsum('bkgqs,bskd->bqkgd', p, v)

    o = lax.map(one, qb)
    return jnp.moveaxis(o, 0, 1).reshape(b, n, ATTN_WIDTH)


def _latent_attention(q, k, v, kc, vc, sink):
    b, n = q.shape[:2]
    nb = n // BLOCK
    span = BLOCK + 2 * WINDOW
    qb = jnp.moveaxis(q.reshape(b, nb, BLOCK, N_KV_HEADS, Q_GROUP, HEAD_DIM), 1, 0)
    kp = jnp.pad(k, ((0, 0), (WINDOW, WINDOW), (0, 0), (0, 0)))
    vp = jnp.pad(v, ((0, 0), (WINDOW, WINDOW), (0, 0), (0, 0)))
    scale = HEAD_DIM ** -0.5
    offs_q = jnp.arange(BLOCK)
    offs_k = jnp.arange(span)

    def one(args):
        i, qi = args
        start = i * BLOCK
        kw = lax.dynamic_slice_in_dim(kp, start, span, axis=1)
        vw = lax.dynamic_slice_in_dim(vp, start, span, axis=1)
        qpos = start + offs_q
        kpos = start - WINDOW + offs_k
        valid = ((jnp.abs(qpos[:, None] - kpos[None, :]) <= WINDOW)
                 & (kpos >= 0)[None, :] & (kpos < n)[None, :])
        s_w = jnp.einsum('bqkgd,bskd->bkgqs', qi, kw).astype(jnp.float32) * scale
        s_w = jnp.where(valid, s_w, NEG_INF)
        s_c = jnp.einsum('bqkgd,bskd->bkgqs', qi, kc).astype(jnp.float32) * scale
        p = _softmax_with_sink(jnp.concatenate([s_w, s_c], axis=-1), sink).astype(v.dtype)
        return (jnp.einsum('bkgqs,bskd->bqkgd', p[..., :span], vw)
                + jnp.einsum('bkgqs,bskd->bqkgd', p[..., span:], vc))

    o = lax.map(one, (jnp.arange(nb), qb))
    return jnp.moveaxis(o, 0, 1).reshape(b, n, ATTN_WIDTH)


def _layer(x, cond, p, cache_k=None, cache_v=None):
    sh1, sc1, g1, sh2, sc2, g2, sh3, sc3, g3 = _modulations(cond, p['mod_w'], p['mod_b'])
    h = _rms(x, p['norm_g'][0]) * (1.0 + sc1) + sh1
    x = x + 0.5 * g1 * _swiglu(h, p['ffn1_wi'], p['ffn1_wo'])
    h = _rms(x, p['norm_g'][1]) * (1.0 + sc2) + sh2
    u = h @ p['w_in']
    o_pool = _pool_mixer(u[..., :POOL_WIDTH], p['pool_w'], p['pool_scale'])
    o_conv = _conv_mixer(u[..., POOL_WIDTH:ATTN_OFFSET], p['conv_dw'], p['conv_b'],
                         p['conv_norm_g'], p['conv_pw'])
    q, k, v = _qkv(u[..., ATTN_OFFSET:], p['q_norm_g'], p['k_norm_g'])
    if cache_k is None:
        o_attn = _context_attention(q, k, v, p['sink'])
        new_kv = (k, v)
    else:
        n = x.shape[1]
        o_attn = _latent_attention(_rope_2d(q, n), _rope_2d(k, n), v, cache_k, cache_v, p['sink'])
        new_kv = None
    x = x + g2 * (jnp.concatenate([o_pool, o_conv, o_attn], axis=-1) @ p['w_out'])
    h = _rms(x, p['norm_g'][2]) * (1.0 + sc3) + sh3
    x = x + 0.5 * g3 * _swiglu(h, p['ffn2_wi'], p['ffn2_wo'])
    return x, new_kv


def setup_inputs(seed: int = 0) -> dict:
    key = jax.random.key(seed)
    ks = jax.random.split(key, 26)
    D = D_MODEL

    def nrm(k, shape, s):
        return jax.random.normal(k, shape, jnp.float32) * s

    cache_shape = (DEC_BATCH, DEPTH, PAST_LEN, N_KV_HEADS, HEAD_DIM)
    return {
        'x_prompt': nrm(ks[0], (BATCH, SEQ, D), 1.0),
        'x_sample': nrm(ks[1], (DEC_BATCH, DEC_SEQ, D), 1.0),
        'cache_k': nrm(ks[2], cache_shape, 1.0),
        'cache_v': nrm(ks[3], cache_shape, 1.0),
        'c': nrm(ks[4], (DEC_BATCH, D), 1.0),
        'c_ctx': nrm(ks[5], (D,), 1.0),
        'mod_w': nrm(ks[6], (DEPTH, D, N_MOD * D), 0.5 * D ** -0.5),
        'mod_b': nrm(ks[7], (DEPTH, N_MOD * D), 0.02),
        'norm_g': 1.0 + nrm(ks[8], (DEPTH, 3, D), 0.02),
        'ffn1_wi': nrm(ks[9], (DEPTH, D, 2 * D_FF), D ** -0.5),
        'ffn1_wo': nrm(ks[10], (DEPTH, D_FF, D), D_FF ** -0.5),
        'ffn2_wi': nrm(ks[11], (DEPTH, D, 2 * D_FF), D ** -0.5),
        'ffn2_wo': nrm(ks[12], (DEPTH, D_FF, D), D_FF ** -0.5),
        'w_in': nrm(ks[13], (DEPTH, D, IN_WIDTH), D ** -0.5),
        'w_out': nrm(ks[14], (DEPTH, MIX_WIDTH, D), MIX_WIDTH ** -0.5),
        'pool_w': nrm(ks[15], (DEPTH, POOL_GROUPS, POOL_GROUP_DIM, POOL_GROUP_DIM), POOL_GROUP_DIM ** -0.5),
        'pool_scale': 1.0 + nrm(ks[16], (DEPTH, POOL_WIDTH), 0.1),
        'conv_dw': nrm(ks[17], (DEPTH, CONV_TAPS, CONV_WIDTH), CONV_TAPS ** -0.5),
        'conv_b': nrm(ks[18], (DEPTH, CONV_WIDTH), 0.02),
        'conv_norm_g': 1.0 + nrm(ks[19], (DEPTH, CONV_WIDTH), 0.02),
        'conv_pw': nrm(ks[20], (DEPTH, CONV_WIDTH, CONV_WIDTH), CONV_WIDTH ** -0.5),
        'q_norm_g': 1.0 + nrm(ks[21], (DEPTH, HEAD_DIM), 0.02),
        'k_norm_g': 1.0 + nrm(ks[22], (DEPTH, HEAD_DIM), 0.02),
        'sink': nrm(ks[23], (DEPTH, N_HEADS), 0.5),
    }


def reference(x_prompt, x_sample, cache_k, cache_v, c, c_ctx, mod_w, mod_b, norm_g,
              ffn1_wi, ffn1_wo, ffn2_wi, ffn2_wo, w_in, w_out, pool_w, pool_scale,
              conv_dw, conv_b, conv_norm_g, conv_pw, q_norm_g, k_norm_g, sink):
    y_prompt = x_prompt
    y_sample = x_sample
    ks, vs = [], []
    for l in range(DEPTH):
        p = {
            'mod_w': mod_w[l], 'mod_b': mod_b[l], 'norm_g': norm_g[l],
            'ffn1_wi': ffn1_wi[l], 'ffn1_wo': ffn1_wo[l],
            'ffn2_wi': ffn2_wi[l], 'ffn2_wo': ffn2_wo[l],
            'w_in': w_in[l], 'w_out': w_out[l],
            'pool_w': pool_w[l], 'pool_scale': pool_scale[l],
            'conv_dw': conv_dw[l], 'conv_b': conv_b[l], 'conv_norm_g': conv_norm_g[l],
            'conv_pw': conv_pw[l], 'q_norm_g': q_norm_g[l], 'k_norm_g': k_norm_g[l],
            'sink': sink[l],
        }
        y_prompt, (k_l, v_l) = _layer(y_prompt, c_ctx[None, :], p)
        ks.append(k_l)
        vs.append(v_l)
        y_sample, _ = _layer(y_sample, c, p, cache_k[:, l], cache_v[:, l])
    new_cache_k = jnp.stack(ks, axis=1)
    new_cache_v = jnp.stack(vs, axis=1)
    return (y_prompt, y_sample, new_cache_k, new_cache_v)
```

```python
import functools

import jax
import jax.numpy as jnp
from jax import lax
from jax.experimental import pallas as pl
from jax.experimental.pallas import tpu as pltpu

D = 1024
DEPTH = 2
GRID_W = 64
POOL_WIDTH = 256
POOL_GROUPS = 4
POOL_GROUP_DIM = POOL_WIDTH // POOL_GROUPS
CONV_WIDTH = 256
CONV_TAPS = 31
N_HEADS = 8
N_KV_HEADS = 2
HEAD_DIM = 64
Q_GROUP = N_HEADS // N_KV_HEADS
ATTN_WIDTH = N_HEADS * HEAD_DIM
KV_WIDTH = N_KV_HEADS * HEAD_DIM
MIX_WIDTH = POOL_WIDTH + CONV_WIDTH + ATTN_WIDTH
ATTN_OFFSET = POOL_WIDTH + 2 * CONV_WIDTH
IN_WIDTH = ATTN_OFFSET + ATTN_WIDTH + 2 * KV_WIDTH
BLOCK = 128
D_FF = 2816
N_MOD = 9
ROPE_THETA = 10000.0
EPS = 1e-6
NEG_INF = -1e30
SCALE = HEAD_DIM ** -0.5

COND_ROWS = 8
HALO = 16
TM_FFN = 512
TM_PROJ = 512
FFN_CHUNKS = 2
VMEM_LIMIT = 56 * 1024 * 1024

F32 = jnp.float32
BF16 = jnp.bfloat16


def _dot(a, b):
    return jnp.dot(a, b, preferred_element_type=F32)


def _rms_mod(x, g, sc, sh):
    ms = jnp.mean(x * x, axis=-1, keepdims=True)
    return (x * lax.rsqrt(ms + EPS) * g) * (1.0 + sc) + sh


def _mod_kernel(cond_ref, w_ref, b_ref, o_ref):
    c = cond_ref[...]
    s = (c * jax.nn.sigmoid(c)).astype(BF16)
    o_ref[...] = _dot(s, w_ref[...].astype(BF16)) + b_ref[...]


def _modulations(cond, mod_w, mod_b):
    tn = 1024
    nt = (N_MOD * D) // tn
    return pl.pallas_call(
        _mod_kernel,
        grid=(DEPTH, nt),
        in_specs=[
            pl.BlockSpec((COND_ROWS, D), lambda l, j: (0, 0)),
            pl.BlockSpec((None, D, tn), lambda l, j: (l, 0, j)),
            pl.BlockSpec((None, 1, tn), lambda l, j: (l, 0, j)),
        ],
        out_specs=pl.BlockSpec((None, COND_ROWS, tn), lambda l, j: (l, 0, j)),
        out_shape=jax.ShapeDtypeStruct((DEPTH, COND_ROWS, N_MOD * D), F32),
        compiler_params=pltpu.CompilerParams(
            dimension_semantics=("arbitrary", "arbitrary"),
            vmem_limit_bytes=VMEM_LIMIT),
        name="modulations",
    )(cond, mod_w, mod_b.reshape(DEPTH, 1, N_MOD * D))


def _cond_row(i, tm, tokens_per_cond, first_row):
    if tokens_per_cond is None:
        return first_row
    return first_row + i // (tokens_per_cond // tm)


def _mod_spec(layer, which, tm, tokens_per_cond, first_row):
    return pl.BlockSpec(
        (None, None, 1, 3 * D),
        lambda i: (layer, _cond_row(i, tm, tokens_per_cond, first_row), 0, which))


def _const_spec(shape, index):
    return pl.BlockSpec(shape, lambda i: index, pipeline_mode=pl.Buffered(1))


def _ffn_kernel(*refs, pre_outproj):
    if pre_outproj:
        x_ref, mix_ref, wout_ref, modp_ref, mod_ref, g_ref, wi_ref, wo_ref, o_ref = refs
        x = x_ref[...] + modp_ref[:, 2 * D:3 * D] * _dot(mix_ref[...], wout_ref[...])
    else:
        x_ref, mod_ref, g_ref, wi_ref, wo_ref, o_ref = refs
        x = x_ref[...]
    h = _rms_mod(x, g_ref[...], mod_ref[:, D:2 * D], mod_ref[:, 0:D]).astype(BF16)
    tf = D_FF // FFN_CHUNKS
    acc = None
    for c in range(FFN_CHUNKS):
        gate = _dot(h, wi_ref[:, c * tf:(c + 1) * tf])
        up = _dot(h, wi_ref[:, D_FF + c * tf:D_FF + (c + 1) * tf])
        a = (gate * jax.nn.sigmoid(gate) * up).astype(BF16)
        part = _dot(a, wo_ref[c * tf:(c + 1) * tf, :])
        acc = part if acc is None else acc + part
    o_ref[...] = x + (0.5 * mod_ref[:, 2 * D:3 * D]) * acc


def _ffn(x, mods, norm_g, wi, wo, layer, which, tokens_per_cond, first_row,
         mix=None, w_out=None):
    t = x.shape[0]
    tm = TM_FFN
    tok = pl.BlockSpec((tm, D), lambda i: (i, 0))
    in_specs = [tok]
    args = [x]
    if mix is not None:
        in_specs += [tok, _const_spec((None, MIX_WIDTH, D), (layer, 0, 0)),
                     _mod_spec(layer, 1, tm, tokens_per_cond, first_row)]
        args += [mix, w_out, mods]
    in_specs += [
        _mod_spec(layer, which, tm, tokens_per_cond, first_row),
        _const_spec((None, None, 1, D), (layer, which, 0, 0)),
        _const_spec((None, D, 2 * D_FF), (layer, 0, 0)),
        _const_spec((None, D_FF, D), (layer, 0, 0)),
    ]
    args += [mods, norm_g, wi, wo]
    return pl.pallas_call(
        functools.partial(_ffn_kernel, pre_outproj=mix is not None),
        grid=(t // tm,),
        in_specs=in_specs,
        out_specs=tok,
        out_shape=jax.ShapeDtypeStruct((t, D), F32),
        compiler_params=pltpu.CompilerParams(
            dimension_semantics=("arbitrary",), vmem_limit_bytes=VMEM_LIMIT),
        name="ffn_outproj" if mix is not None else "ffn",
    )(*args)


def _head_norm(x, m, g):
    sq = x * x
    hi = sq.astype(BF16)
    lo = (sq - hi.astype(F32)).astype(BF16)
    ms = _dot(hi, m) + _dot(lo, m)
    return x * lax.rsqrt(ms + EPS) * g


def _rope(x, c, s_up, s_dn):
    n = x.shape[1]
    return x * c + pltpu.roll(x, n - 16, 1) * s_up + pltpu.roll(x, 16, 1) * s_dn


def _proj_kernel(*refs, rope):
    if rope:
        (x_ref, mod_ref, g_ref, win_ref, mq_ref, mk_ref, qg_ref, kg_ref,
         c_ref, su_ref, sd_ref, pool_ref, glu_ref, q_ref, k_ref, v_ref) = refs
    else:
        (x_ref, mod_ref, g_ref, win_ref, mq_ref, mk_ref, qg_ref, kg_ref,
         pool_ref, glu_ref, q_ref, k_ref, v_ref) = refs
    h = _rms_mod(x_ref[...], g_ref[...], mod_ref[:, D:2 * D], mod_ref[:, 0:D]).astype(BF16)
    u = _dot(h, win_ref[...])
    pool_ref[...] = u[:, :POOL_WIDTH]
    glu_ref[...] = (u[:, POOL_WIDTH:POOL_WIDTH + CONV_WIDTH]
                    * jax.nn.sigmoid(u[:, POOL_WIDTH + CONV_WIDTH:ATTN_OFFSET]))
    q = _head_norm(u[:, ATTN_OFFSET:ATTN_OFFSET + ATTN_WIDTH], mq_ref[...], qg_ref[...])
    k = _head_norm(u[:, ATTN_OFFSET + ATTN_WIDTH:ATTN_OFFSET + ATTN_WIDTH + KV_WIDTH],
                   mk_ref[...], kg_ref[...])
    if rope:
        c, su, sd = c_ref[...], su_ref[...], sd_ref[...]
        k = _rope(k, c, su, sd)
        rep = ATTN_WIDTH // KV_WIDTH
        q = _rope(q, jnp.concatenate([c] * rep, axis=1), jnp.concatenate([su] * rep, axis=1),
                  jnp.concatenate([sd] * rep, axis=1))
    q_ref[...] = (q * SCALE).astype(BF16)
    k_ref[...] = k
    v_ref[...] = u[:, ATTN_OFFSET + ATTN_WIDTH + KV_WIDTH:]


def _proj(x, mods, norm_g, w_in, mq, mk, qg, kg, layer, tokens_per_cond, first_row,
          rope_tables=None):
    t = x.shape[0]
    tm = TM_PROJ
    in_specs = [
        pl.BlockSpec((tm, D), lambda i: (i, 0)),
        _mod_spec(layer, 1, tm, tokens_per_cond, first_row),
        _const_spec((None, None, 1, D), (layer, 1, 0, 0)),
        _const_spec((None, D, IN_WIDTH), (layer, 0, 0)),
        _const_spec((ATTN_WIDTH, ATTN_WIDTH), (0, 0)),
        _const_spec((KV_WIDTH, KV_WIDTH), (0, 0)),
        _const_spec((None, 1, ATTN_WIDTH), (layer, 0, 0)),
        _const_spec((None, 1, KV_WIDTH), (layer, 0, 0)),
    ]
    args = [x, mods, norm_g, w_in, mq, mk, qg, kg]
    if rope_tables is not None:
        seq_tiles = rope_tables[0].shape[0] // tm
        in_specs += [pl.BlockSpec((tm, KV_WIDTH), lambda i: (i % seq_tiles, 0))] * 3
        args += list(rope_tables)

    def out(width):
        return pl.BlockSpec((tm, width), lambda i: (i, 0))

    return pl.pallas_call(
        functools.partial(_proj_kernel, rope=rope_tables is not None),
        grid=(t // tm,),
        in_specs=in_specs,
        out_specs=[out(POOL_WIDTH), out(CONV_WIDTH), out(ATTN_WIDTH), out(KV_WIDTH), out(KV_WIDTH)],
        out_shape=[
            jax.ShapeDtypeStruct((t, POOL_WIDTH), F32),
            jax.ShapeDtypeStruct((t, CONV_WIDTH), F32),
            jax.ShapeDtypeStruct((t, ATTN_WIDTH), BF16),
            jax.ShapeDtypeStruct((t, KV_WIDTH), F32),
            jax.ShapeDtypeStruct((t, KV_WIDTH), F32),
        ],
        compiler_params=pltpu.CompilerParams(
            dimension_semantics=("arbitrary",), vmem_limit_bytes=VMEM_LIMIT),
        name="proj_rope" if rope_tables is not None else "proj",
    )(*args)


def _fill_ext(ext_ref, prev_ref, cur_ref, next_ref, i, nb):
    ext_ref[0:HALO, :] = jnp.where(i > 0, prev_ref[...], 0.0)
    ext_ref[HALO:HALO + BLOCK, :] = cur_ref[...]
    ext_ref[HALO + BLOCK:, :] = jnp.where(i < nb - 1, next_ref[...], 0.0)


def _pool_mix(ext_ref, i, n, poolw_ref, pscale_ref):
    def u(off):
        return ext_ref[HALO + off:HALO + off + BLOCK, :]

    cur = u(0)
    acc2 = u(-1) + cur
    acc4 = acc2 + u(-2) + u(1)
    acc8 = acc4 + u(-4) + u(-3) + u(2) + u(3)
    acc16 = acc8 + u(-8) + u(-7) + u(-6) + u(-5) + u(4) + u(5) + u(6) + u(7)
    lane = lax.broadcasted_iota(jnp.int32, (BLOCK, POOL_WIDTH), 1)
    t = i * BLOCK + lax.broadcasted_iota(jnp.int32, (BLOCK, POOL_WIDTH), 0)
    group = lane // POOL_GROUP_DIM
    total = jnp.where(group == 0, acc2, jnp.where(group == 1, acc4, jnp.where(group == 2, acc8, acc16)))
    half = jnp.where(group == 0, 1, jnp.where(group == 1, 2, jnp.where(group == 2, 4, 8)))
    count = jnp.minimum(t + half, n) - jnp.maximum(t - half, 0)
    pooled = total / count.astype(F32) - cur
    return _dot(pooled.astype(BF16), poolw_ref[...]) * pscale_ref[...]


def _conv_mix(ext_ref, dw_ref, cb_ref, cg_ref, pw_ref):
    acc = None
    for k in range(CONV_TAPS):
        off = HALO - CONV_TAPS // 2 + k
        term = ext_ref[off:off + BLOCK, :] * dw_ref[k:k + 1, :]
        acc = term if acc is None else acc + term
    y = acc + cb_ref[...]
    ms = jnp.mean(y * y, axis=-1, keepdims=True)
    z = y * lax.rsqrt(ms + EPS) * cg_ref[...]
    z = z * jax.nn.sigmoid(z)
    return _dot(z.astype(BF16), pw_ref[...])


def _attend(q_ref, k_all, v_all, sink_ref, layer, bias_fn):
    kb = k_all.astype(BF16)
    vb = v_all.astype(BF16)
    outs = []
    for j in range(N_KV_HEADS):
        heads = range(j * Q_GROUP, (j + 1) * Q_GROUP)
        qj = jnp.concatenate([q_ref[:, h * HEAD_DIM:(h + 1) * HEAD_DIM] for h in heads], axis=0)
        kj = kb[:, j * HEAD_DIM:(j + 1) * HEAD_DIM]
        vj = vb[:, j * HEAD_DIM:(j + 1) * HEAD_DIM]
        s = lax.dot_general(qj, kj, (((1,), (1,)), ((), ())), preferred_element_type=F32)
        if bias_fn is not None:
            s = bias_fn(s)
        sink = jnp.concatenate(
            [jnp.full((BLOCK, 1), sink_ref[layer * N_HEADS + h], F32) for h in heads], axis=0)
        m = jnp.maximum(jnp.max(s, axis=-1, keepdims=True), sink)
        p = jnp.exp(s - m)
        denom = jnp.sum(p, axis=-1, keepdims=True) + jnp.exp(sink - m)
        o = _dot(p.astype(BF16), vj) / denom
        outs += [o[g * BLOCK:(g + 1) * BLOCK, :] for g in range(Q_GROUP)]
    return jnp.concatenate(outs, axis=1)


def _mixer_kernel(*refs, layer, latent, n):
    nb = n // BLOCK
    i = pl.program_id(1)
    (pp_ref, pc_ref, pn_ref, gp_ref, gc_ref, gn_ref, q_ref) = refs[:7]
    rest = refs[7:]
    if latent:
        (kp_ref, kc_ref, kn_ref, vp_ref, vc_ref, vn_ref, ck_ref, cv_ref) = rest[:8]
        rest = rest[8:]
    else:
        (ka_ref, va_ref) = rest[:2]
        rest = rest[2:]
    (poolw_ref, pscale_ref, dw_ref, cb_ref, cg_ref, pw_ref, sink_ref,
     o_ref, pext_ref, gext_ref) = rest

    _fill_ext(pext_ref, pp_ref, pc_ref, pn_ref, i, nb)
    o_ref[:, 0:POOL_WIDTH] = _pool_mix(pext_ref, i, n, poolw_ref, pscale_ref).astype(BF16)

    _fill_ext(gext_ref, gp_ref, gc_ref, gn_ref, i, nb)
    o_ref[:, POOL_WIDTH:POOL_WIDTH + CONV_WIDTH] = _conv_mix(
        gext_ref, dw_ref, cb_ref, cg_ref, pw_ref).astype(BF16)

    if latent:
        k_all = jnp.concatenate([kp_ref[...], kc_ref[...], kn_ref[...], ck_ref[...]], axis=0)
        v_all = jnp.concatenate([vp_ref[...], vc_ref[...], vn_ref[...], cv_ref[...]], axis=0)
        rows = Q_GROUP * BLOCK
        r = lax.broadcasted_iota(jnp.int32, (rows, BLOCK), 0) % BLOCK
        col = lax.broadcasted_iota(jnp.int32, (rows, BLOCK), 1)
        b_prev = jnp.where((col >= r) & (i > 0), 0.0, NEG_INF).astype(F32)
        b_next = jnp.where((col <= r) & (i < nb - 1), 0.0, NEG_INF).astype(F32)

        def bias_fn(s):
            return jnp.concatenate(
                [s[:, 0:BLOCK] + b_prev, s[:, BLOCK:2 * BLOCK],
                 s[:, 2 * BLOCK:3 * BLOCK] + b_next, s[:, 3 * BLOCK:]], axis=1)
    else:
        k_all, v_all, bias_fn = ka_ref[...], va_ref[...], None
    attn = _attend(q_ref, k_all, v_all, sink_ref, layer, bias_fn)
    o_ref[:, POOL_WIDTH + CONV_WIDTH:] = attn.astype(BF16)


def _mixer(pool_in, glu, q, k, v, wts, layer, cache=None):
    b, n, _ = pool_in.shape
    nb = n // BLOCK
    hpb = BLOCK // HALO
    nh = n // HALO

    def cur(width):
        return pl.BlockSpec((None, BLOCK, width), lambda bi, i: (bi, i, 0))

    def halo_prev(width):
        return pl.BlockSpec((None, HALO, width), lambda bi, i: (bi, jnp.maximum(i * hpb - 1, 0), 0))

    def halo_next(width):
        return pl.BlockSpec((None, HALO, width),
                            lambda bi, i: (bi, jnp.minimum((i + 1) * hpb, nh - 1), 0))

    def const(shape, index):
        return pl.BlockSpec(shape, lambda bi, i: index)

    in_specs = [halo_prev(POOL_WIDTH), cur(POOL_WIDTH), halo_next(POOL_WIDTH),
                halo_prev(CONV_WIDTH), cur(CONV_WIDTH), halo_next(CONV_WIDTH),
                cur(ATTN_WIDTH)]
    args = [pool_in, pool_in, pool_in, glu, glu, glu, q]
    if cache is not None:
        cache_k, cache_v = cache
        past = cache_k.shape[2]
        blk_prev = pl.BlockSpec((None, BLOCK, KV_WIDTH), lambda bi, i: (bi, jnp.maximum(i - 1, 0), 0))
        blk_next = pl.BlockSpec((None, BLOCK, KV_WIDTH),
                                lambda bi, i: (bi, jnp.minimum(i + 1, nb - 1), 0))
        cspec = pl.BlockSpec((None, None, past, KV_WIDTH), lambda bi, i: (bi, layer, 0, 0))
        in_specs += [blk_prev, cur(KV_WIDTH), blk_next, blk_prev, cur(KV_WIDTH), blk_next,
                     cspec, cspec]
        args += [k, k, k, v, v, v, cache_k, cache_v]
    else:
        whole = pl.BlockSpec((None, n, KV_WIDTH), lambda bi, i: (bi, 0, 0))
        in_specs += [whole, whole]
        args += [k, v]
    in_specs += [
        const((None, POOL_WIDTH, POOL_WIDTH), (layer, 0, 0)),
        const((None, 1, POOL_WIDTH), (layer, 0, 0)),
        const((None, CONV_TAPS, CONV_WIDTH), (layer, 0, 0)),
        const((None, 1, CONV_WIDTH), (layer, 0, 0)),
        const((None, 1, CONV_WIDTH), (layer, 0, 0)),
        const((None, CONV_WIDTH, CONV_WIDTH), (layer, 0, 0)),
        pl.BlockSpec(memory_space=pltpu.SMEM),
    ]
    args += [wts["pool_bd"], wts["pool_scale"], wts["conv_dw"], wts["conv_b"],
             wts["conv_norm_g"], wts["conv_pw"], wts["sink"]]
    return pl.pallas_call(
        functools.partial(_mixer_kernel, layer=layer, latent=cache is not None, n=n),
        grid=(b, nb),
        in_specs=in_specs,
        out_specs=pl.BlockSpec((None, BLOCK, MIX_WIDTH), lambda bi, i: (bi, i, 0)),
        out_shape=jax.ShapeDtypeStruct((b, n, MIX_WIDTH), BF16),
        scratch_shapes=[pltpu.VMEM((BLOCK + 2 * HALO, POOL_WIDTH), F32),
                        pltpu.VMEM((BLOCK + 2 * HALO, CONV_WIDTH), F32)],
        compiler_params=pltpu.CompilerParams(
            dimension_semantics=("arbitrary", "arbitrary"), vmem_limit_bytes=VMEM_LIMIT),
        name="mixer_latent" if cache is not None else "mixer_context",
    )(*args)


def _rope_tables(n):
    t = jnp.arange(n)
    row = (t // GRID_W).astype(F32)
    col = (t % GRID_W).astype(F32)
    half = HEAD_DIM // 2
    inv = ROPE_THETA ** (-jnp.arange(0, half, 2, dtype=F32) / half)
    zeros = jnp.zeros((n, half // 2), F32)
    cos, s_up, s_dn = [], [], []
    for pos in (row, col):
        ang = pos[:, None] * inv[None, :]
        cos += [jnp.cos(ang), jnp.cos(ang)]
        s_up += [-jnp.sin(ang), zeros]
        s_dn += [zeros, jnp.sin(ang)]
    return tuple(jnp.concatenate(parts * N_KV_HEADS, axis=1) for parts in (cos, s_up, s_dn))


def _block_diag_mean(width):
    idx = jnp.arange(width) // HEAD_DIM
    return jnp.where(idx[:, None] == idx[None, :], 1.0 / HEAD_DIM, 0.0).astype(BF16)


def kernel(x_prompt, x_sample, cache_k, cache_v, c, c_ctx, mod_w, mod_b, norm_g, ffn1_wi, ffn1_wo, ffn2_wi, ffn2_wo, w_in, w_out, pool_w, pool_scale, conv_dw, conv_b, conv_norm_g, conv_pw, q_norm_g, k_norm_g, sink):
    batch, seq, _ = x_prompt.shape
    dec_batch, dec_seq, _ = x_sample.shape
    past = cache_k.shape[2]
    assert 1 + dec_batch <= COND_ROWS

    cond = jnp.zeros((COND_ROWS, D), F32).at[0].set(c_ctx).at[1:1 + dec_batch].set(c)
    mods = _modulations(cond, mod_w, mod_b).reshape(DEPTH, COND_ROWS, 1, N_MOD * D)

    norm_g4 = norm_g.reshape(DEPTH, 3, 1, D)
    wi1, wo1 = ffn1_wi.astype(BF16), ffn1_wo.astype(BF16)
    wi2, wo2 = ffn2_wi.astype(BF16), ffn2_wo.astype(BF16)
    w_in_b, w_out_b = w_in.astype(BF16), w_out.astype(BF16)
    eye = jnp.eye(POOL_GROUPS, dtype=F32)
    pool_bd = (eye[None, :, None, :, None] * pool_w[:, :, :, None, :]).reshape(
        DEPTH, POOL_WIDTH, POOL_WIDTH).astype(BF16)
    wts = {
        "pool_bd": pool_bd,
        "pool_scale": pool_scale.reshape(DEPTH, 1, POOL_WIDTH),
        "conv_dw": conv_dw,
        "conv_b": conv_b.reshape(DEPTH, 1, CONV_WIDTH),
        "conv_norm_g": conv_norm_g.reshape(DEPTH, 1, CONV_WIDTH),
        "conv_pw": conv_pw.astype(BF16),
        "sink": sink.reshape(DEPTH * N_HEADS),
    }
    mq, mk = _block_diag_mean(ATTN_WIDTH), _block_diag_mean(KV_WIDTH)
    qg = jnp.tile(q_norm_g, (1, N_HEADS)).reshape(DEPTH, 1, ATTN_WIDTH)
    kg = jnp.tile(k_norm_g, (1, N_KV_HEADS)).reshape(DEPTH, 1, KV_WIDTH)
    rope = _rope_tables(dec_seq)
    cache_k4 = cache_k.reshape(dec_batch, DEPTH, past, KV_WIDTH)
    cache_v4 = cache_v.reshape(dec_batch, DEPTH, past, KV_WIDTH)

    def layer_fn(x, layer, bsz, n, tokens_per_cond, first_row, latent):
        x = _ffn(x, mods, norm_g4, wi1, wo1, layer, 0, tokens_per_cond, first_row)
        pool_in, glu, q, k, v = _proj(
            x, mods, norm_g4, w_in_b, mq, mk, qg, kg, layer, tokens_per_cond, first_row,
            rope_tables=rope if latent else None)

        def seqs(a):
            return a.reshape(bsz, n, a.shape[-1])

        mix = _mixer(seqs(pool_in), seqs(glu), seqs(q), seqs(k), seqs(v), wts, layer,
                     cache=(cache_k4, cache_v4) if latent else None)
        x = _ffn(x, mods, norm_g4, wi2, wo2, layer, 2, tokens_per_cond, first_row,
                 mix=mix.reshape(bsz * n, MIX_WIDTH), w_out=w_out_b)
        return x, k, v

    yp = x_prompt.reshape(batch * seq, D)
    ys = x_sample.reshape(dec_batch * dec_seq, D)
    ks, vs = [], []
    for layer in range(DEPTH):
        yp, k_l, v_l = layer_fn(yp, layer, batch, seq, None, 0, False)
        ks.append(k_l.reshape(batch, seq, N_KV_HEADS, HEAD_DIM))
        vs.append(v_l.reshape(batch, seq, N_KV_HEADS, HEAD_DIM))
        ys, _, _ = layer_fn(ys, layer, dec_batch, dec_seq, dec_seq, 1, True)
    return (yp.reshape(batch, seq, D), ys.reshape(dec_batch, dec_seq, D),
            jnp.stack(ks, axis=1), jnp.stack(vs, axis=1))
```

```python
import functools

import jax
import jax.numpy as jnp
from jax import lax
from jax.experimental import pallas as pl
from jax.experimental.pallas import tpu as pltpu

D = 1024
DEPTH = 2
GRID_W = 64
POOL_WIDTH = 256
POOL_GROUPS = 4
POOL_GROUP_DIM = POOL_WIDTH // POOL_GROUPS
CONV_WIDTH = 256
CONV_TAPS = 31
N_HEADS = 8
N_KV_HEADS = 2
HEAD_DIM = 64
Q_GROUP = N_HEADS // N_KV_HEADS
ATTN_WIDTH = N_HEADS * HEAD_DIM
KV_WIDTH = N_KV_HEADS * HEAD_DIM
MIX_WIDTH = POOL_WIDTH + CONV_WIDTH + ATTN_WIDTH
ATTN_OFFSET = POOL_WIDTH + 2 * CONV_WIDTH
IN_WIDTH = ATTN_OFFSET + ATTN_WIDTH + 2 * KV_WIDTH
BLOCK = 128
D_FF = 2816
N_MOD = 9
ROPE_THETA = 10000.0
EPS = 1e-6
NEG_INF = -1e30
SCALE = HEAD_DIM ** -0.5

COND_ROWS = 8
HALO = 16
TM_FFN = 512
TM_PROJ = 512
MXU_TILE = 256
FFN_CHUNK = 2 * MXU_TILE
TB = 256
ROW_CHUNK = 64
ZROWS = TB + 24
VMEM_LIMIT = 56 * 1024 * 1024

F32 = jnp.float32
BF16 = jnp.bfloat16


def _dot(a, b):
    return jnp.dot(a, b, preferred_element_type=F32)


def _rms_mod(x, g, sc, sh):
    ms = jnp.mean(x * x, axis=-1, keepdims=True)
    return (x * lax.rsqrt(ms + EPS) * g) * (1.0 + sc) + sh


def _mod_kernel(cond_ref, w_ref, b_ref, o_ref):
    c = cond_ref[...]
    s = (c * jax.nn.sigmoid(c)).astype(BF16)
    o_ref[...] = _dot(s, w_ref[...].astype(BF16)) + b_ref[...]


def _modulations(cond, mod_w, mod_b):
    tn = 1024
    nt = (N_MOD * D) // tn
    return pl.pallas_call(
        _mod_kernel,
        grid=(DEPTH, nt),
        in_specs=[
            pl.BlockSpec((COND_ROWS, D), lambda l, j: (0, 0)),
            pl.BlockSpec((None, D, tn), lambda l, j: (l, 0, j)),
            pl.BlockSpec((None, 1, tn), lambda l, j: (l, 0, j)),
        ],
        out_specs=pl.BlockSpec((None, COND_ROWS, tn), lambda l, j: (l, 0, j)),
        out_shape=jax.ShapeDtypeStruct((DEPTH, COND_ROWS, N_MOD * D), F32),
        compiler_params=pltpu.CompilerParams(
            dimension_semantics=("arbitrary", "arbitrary"),
            vmem_limit_bytes=VMEM_LIMIT),
        name="modulations",
    )(cond, mod_w, mod_b.reshape(DEPTH, 1, N_MOD * D))


def _cond_row(i, tm, tokens_per_cond, first_row):
    if tokens_per_cond is None:
        return first_row
    return first_row + i // (tokens_per_cond // tm)


def _mod_spec(layer, which, tm, tokens_per_cond, first_row):
    return pl.BlockSpec(
        (None, None, 1, 3 * D),
        lambda i: (layer, _cond_row(i, tm, tokens_per_cond, first_row), 0, which))


def _const_spec(shape, index):
    return pl.BlockSpec(shape, lambda i: index, pipeline_mode=pl.Buffered(1))


def _ffn_kernel(*refs, pre_outproj):
    if pre_outproj:
        x_ref, mix_ref, wout_ref, modp_ref, mod_ref, g_ref, wi_ref, wo_ref, o_ref, a_ref = refs
        x = x_ref[...] + modp_ref[:, 2 * D:3 * D] * _dot(mix_ref[...], wout_ref[...])
    else:
        x_ref, mod_ref, g_ref, wi_ref, wo_ref, o_ref, a_ref = refs
        x = x_ref[...]
    h = _rms_mod(x, g_ref[...], mod_ref[:, D:2 * D], mod_ref[:, 0:D]).astype(BF16)
    for c0 in range(0, D_FF, FFN_CHUNK):
        c1 = min(c0 + FFN_CHUNK, D_FF)
        gate = _dot(h, wi_ref[:, c0:c1])
        up = _dot(h, wi_ref[:, D_FF + c0:D_FF + c1])
        a_ref[:, c0:c1] = (gate * jax.nn.sigmoid(gate) * up).astype(BF16)
    o_ref[...] = x + (0.5 * mod_ref[:, 2 * D:3 * D]) * _dot(a_ref[...], wo_ref[...])


def _ffn(x, mods, norm_g, wi, wo, layer, which, tokens_per_cond, first_row,
         mix=None, w_out=None):
    t = x.shape[0]
    tm = TM_FFN
    tok = pl.BlockSpec((tm, D), lambda i: (i, 0))
    in_specs = [tok]
    args = [x]
    if mix is not None:
        in_specs += [tok, _const_spec((None, MIX_WIDTH, D), (layer, 0, 0)),
                     _mod_spec(layer, 1, tm, tokens_per_cond, first_row)]
        args += [mix, w_out, mods]
    in_specs += [
        _mod_spec(layer, which, tm, tokens_per_cond, first_row),
        _const_spec((None, None, 1, D), (layer, which, 0, 0)),
        _const_spec((None, D, 2 * D_FF), (layer, 0, 0)),
        _const_spec((None, D_FF, D), (layer, 0, 0)),
    ]
    args += [mods, norm_g, wi, wo]
    return pl.pallas_call(
        functools.partial(_ffn_kernel, pre_outproj=mix is not None),
        grid=(t // tm,),
        in_specs=in_specs,
        out_specs=tok,
        out_shape=jax.ShapeDtypeStruct((t, D), F32),
        scratch_shapes=[pltpu.VMEM((tm, D_FF), BF16)],
        compiler_params=pltpu.CompilerParams(
            dimension_semantics=("arbitrary",), vmem_limit_bytes=VMEM_LIMIT),
        name="ffn_outproj" if mix is not None else "ffn",
    )(*args)


def _head_norm(x, m, g):
    sq = x * x
    hi = sq.astype(BF16)
    lo = (sq - hi.astype(F32)).astype(BF16)
    ms = _dot(hi, m) + _dot(lo, m)
    return x * lax.rsqrt(ms + EPS) * g


def _rope(x, c, s_up, s_dn):
    n = x.shape[1]
    return x * c + pltpu.roll(x, n - 16, 1) * s_up + pltpu.roll(x, 16, 1) * s_dn


def _proj_kernel(*refs, rope):
    if rope:
        (x_ref, mod_ref, g_ref, win_ref, mq_ref, mk_ref, qg_ref, kg_ref,
         c_ref, su_ref, sd_ref, pool_ref, glu_ref, q_ref, k_ref, v_ref) = refs
    else:
        (x_ref, mod_ref, g_ref, win_ref, mq_ref, mk_ref, qg_ref, kg_ref,
         pool_ref, glu_ref, q_ref, k_ref, v_ref) = refs
    h = _rms_mod(x_ref[...], g_ref[...], mod_ref[:, D:2 * D], mod_ref[:, 0:D]).astype(BF16)
    u = _dot(h, win_ref[...])
    pool_ref[...] = u[:, :POOL_WIDTH]
    glu_ref[...] = (u[:, POOL_WIDTH:POOL_WIDTH + CONV_WIDTH]
                    * jax.nn.sigmoid(u[:, POOL_WIDTH + CONV_WIDTH:ATTN_OFFSET]))
    q = _head_norm(u[:, ATTN_OFFSET:ATTN_OFFSET + ATTN_WIDTH], mq_ref[...], qg_ref[...])
    k = _head_norm(u[:, ATTN_OFFSET + ATTN_WIDTH:ATTN_OFFSET + ATTN_WIDTH + KV_WIDTH],
                   mk_ref[...], kg_ref[...])
    if rope:
        c, su, sd = c_ref[...], su_ref[...], sd_ref[...]
        k = _rope(k, c, su, sd)
        rep = ATTN_WIDTH // KV_WIDTH
        q = _rope(q, jnp.concatenate([c] * rep, axis=1), jnp.concatenate([su] * rep, axis=1),
                  jnp.concatenate([sd] * rep, axis=1))
    q_ref[...] = (q * SCALE).astype(BF16)
    k_ref[...] = k
    v_ref[...] = u[:, ATTN_OFFSET + ATTN_WIDTH + KV_WIDTH:]


def _proj(x, mods, norm_g, w_in, mq, mk, qg, kg, layer, tokens_per_cond, first_row,
          rope_tables=None):
    t = x.shape[0]
    tm = TM_PROJ
    in_specs = [
        pl.BlockSpec((tm, D), lambda i: (i, 0)),
        _mod_spec(layer, 1, tm, tokens_per_cond, first_row),
        _const_spec((None, None, 1, D), (layer, 1, 0, 0)),
        _const_spec((None, D, IN_WIDTH), (layer, 0, 0)),
        _const_spec((ATTN_WIDTH, ATTN_WIDTH), (0, 0)),
        _const_spec((KV_WIDTH, KV_WIDTH), (0, 0)),
        _const_spec((None, 1, ATTN_WIDTH), (layer, 0, 0)),
        _const_spec((None, 1, KV_WIDTH), (layer, 0, 0)),
    ]
    args = [x, mods, norm_g, w_in, mq, mk, qg, kg]
    if rope_tables is not None:
        seq_tiles = rope_tables[0].shape[0] // tm
        in_specs += [pl.BlockSpec((tm, KV_WIDTH), lambda i: (i % seq_tiles, 0))] * 3
        args += list(rope_tables)

    def out(width):
        return pl.BlockSpec((tm, width), lambda i: (i, 0))

    return pl.pallas_call(
        functools.partial(_proj_kernel, rope=rope_tables is not None),
        grid=(t // tm,),
        in_specs=in_specs,
        out_specs=[out(POOL_WIDTH), out(CONV_WIDTH), out(ATTN_WIDTH), out(KV_WIDTH), out(KV_WIDTH)],
        out_shape=[
            jax.ShapeDtypeStruct((t, POOL_WIDTH), F32),
            jax.ShapeDtypeStruct((t, CONV_WIDTH), F32),
            jax.ShapeDtypeStruct((t, ATTN_WIDTH), BF16),
            jax.ShapeDtypeStruct((t, KV_WIDTH), F32),
            jax.ShapeDtypeStruct((t, KV_WIDTH), F32),
        ],
        compiler_params=pltpu.CompilerParams(
            dimension_semantics=("arbitrary",), vmem_limit_bytes=VMEM_LIMIT),
        name="proj_rope" if rope_tables is not None else "proj",
    )(*args)


def _fill_ext(ext_ref, prev_ref, cur_ref, next_ref, i, nb):
    ext_ref[0:HALO, :] = jnp.where(i > 0, prev_ref[...], 0.0)
    ext_ref[HALO:HALO + TB, :] = cur_ref[...]
    ext_ref[HALO + TB:, :] = jnp.where(i < nb - 1, next_ref[...], 0.0)


def _shift_copies(ext_ref, z_ref, shifts, lanes):
    for b in shifts:
        z_ref[b, :, lanes] = ext_ref[b:b + ZROWS, lanes]


def _window(z_ref, ext_off, r0, lanes):
    a, b = divmod(ext_off, 8)
    return z_ref[b, 8 * a + r0:8 * a + r0 + ROW_CHUNK, lanes]


def _pool_mix(z_ref, pooled_ref, i, n, poolw_ref, pscale_ref):
    lo_lanes, hi_lanes = slice(0, 128), slice(128, 256)
    lane = lax.broadcasted_iota(jnp.int32, (ROW_CHUNK, 128), 1)
    first = lane < POOL_GROUP_DIM
    for r0 in range(0, TB, ROW_CHUNK):
        t = i * TB + r0 + lax.broadcasted_iota(jnp.int32, (ROW_CHUNK, 128), 0)

        def centred(total, half, cur):
            count = jnp.minimum(t + half, n) - jnp.maximum(t - half, 0)
            return (total / count.astype(F32) - cur).astype(BF16)

        def u(off, lanes):
            return _window(z_ref, HALO + off, r0, lanes)

        cur = u(0, lo_lanes)
        acc2 = u(-1, lo_lanes) + cur
        acc4 = acc2 + u(-2, lo_lanes) + u(1, lo_lanes)
        pooled_ref[r0:r0 + ROW_CHUNK, lo_lanes] = centred(
            jnp.where(first, acc2, acc4), jnp.where(first, 1, 2), cur)
        cur = u(0, hi_lanes)
        acc8 = cur
        for off in (-4, -3, -2, -1, 1, 2, 3):
            acc8 = acc8 + u(off, hi_lanes)
        acc16 = acc8
        for off in (-8, -7, -6, -5, 4, 5, 6, 7):
            acc16 = acc16 + u(off, hi_lanes)
        pooled_ref[r0:r0 + ROW_CHUNK, hi_lanes] = centred(
            jnp.where(first, acc8, acc16), jnp.where(first, 4, 8), cur)
    return _dot(pooled_ref[...], poolw_ref[...]) * pscale_ref[...]


def _conv_mix(z_ref, act_ref, dw_ref, cb_ref, cg_ref, pw_ref):
    every = slice(None)
    for r0 in range(0, TB, ROW_CHUNK):
        acc = None
        for k in range(CONV_TAPS):
            term = _window(z_ref, HALO - CONV_TAPS // 2 + k, r0, every) * dw_ref[k:k + 1, :]
            acc = term if acc is None else acc + term
        y = acc + cb_ref[...]
        ms = jnp.mean(y * y, axis=-1, keepdims=True)
        z = y * lax.rsqrt(ms + EPS) * cg_ref[...]
        act_ref[r0:r0 + ROW_CHUNK, :] = (z * jax.nn.sigmoid(z)).astype(BF16)
    return _dot(act_ref[...], pw_ref[...])


def _attend(q_ref, q0, kb, vt, sink_ref, layer, bias_fn):
    lane = lax.broadcasted_iota(jnp.int32, (BLOCK, KV_WIDTH), 1)
    pieces = []
    for j in range(N_KV_HEADS):
        own = (lane // HEAD_DIM) == j
        qs = []
        for h in range(j * Q_GROUP, (j + 1) * Q_GROUP):
            c0 = (h // 2) * KV_WIDTH
            qh = q_ref[q0:q0 + BLOCK, c0:c0 + KV_WIDTH]
            if h % 2 != j:
                qh = jnp.concatenate([qh[:, HEAD_DIM:], qh[:, :HEAD_DIM]], axis=1)
            qs.append(jnp.where(own, qh, jnp.zeros_like(qh)))
        qj = jnp.concatenate(qs, axis=0)
        s = lax.dot_general(kb, qj, (((1,), (1,)), ((), ())), preferred_element_type=F32)
        if bias_fn is not None:
            s = bias_fn(s)
        sink = jnp.concatenate(
            [jnp.full((1, BLOCK), sink_ref[layer * N_HEADS + j * Q_GROUP + g], F32)
             for g in range(Q_GROUP)], axis=1)
        m = jnp.maximum(jnp.max(s, axis=0, keepdims=True), sink)
        p = jnp.exp(s - m)
        denom = jnp.sum(p, axis=0, keepdims=True) + jnp.exp(sink - m)
        o = _dot(vt[j * HEAD_DIM:(j + 1) * HEAD_DIM, :], p.astype(BF16)) / denom
        pieces += [o[:, g * BLOCK:(g + 1) * BLOCK] for g in range(Q_GROUP)]
    return jnp.concatenate(pieces, axis=0).T


def _mixer_kernel(*refs, layer, latent, n):
    nb = n // TB
    i = pl.program_id(1)
    (pp_ref, pc_ref, pn_ref, gp_ref, gc_ref, gn_ref, q_ref) = refs[:7]
    rest = refs[7:]
    if latent:
        (kp_ref, kc_ref, kn_ref, vp_ref, vc_ref, vn_ref, ck_ref, cv_ref) = rest[:8]
        rest = rest[8:]
    else:
        (ka_ref, va_ref) = rest[:2]
        rest = rest[2:]
    (poolw_ref, pscale_ref, dw_ref, cb_ref, cg_ref, pw_ref, sink_ref,
     o_ref, pext_ref, gext_ref, zp_ref, zg_ref, pooled_ref, act_ref) = rest

    _fill_ext(pext_ref, pp_ref, pc_ref, pn_ref, i, nb)
    _shift_copies(pext_ref, zp_ref, (6, 7, 0, 1), slice(0, 128))
    _shift_copies(pext_ref, zp_ref, range(8), slice(128, 256))
    o_ref[:, 0:POOL_WIDTH] = _pool_mix(
        zp_ref, pooled_ref, i, n, poolw_ref, pscale_ref).astype(BF16)

    _fill_ext(gext_ref, gp_ref, gc_ref, gn_ref, i, nb)
    _shift_copies(gext_ref, zg_ref, range(8), slice(None))
    o_ref[:, POOL_WIDTH:POOL_WIDTH + CONV_WIDTH] = _conv_mix(
        zg_ref, act_ref, dw_ref, cb_ref, cg_ref, pw_ref).astype(BF16)

    if latent:
        k_seq = jnp.concatenate([kp_ref[...], kc_ref[...], kn_ref[...]], axis=0).astype(BF16)
        vt_seq = jnp.concatenate([vp_ref[...], vc_ref[...], vn_ref[...]], axis=0).T.astype(BF16)
        k_ctx, vt_ctx = ck_ref[...].astype(BF16), cv_ref[...].T.astype(BF16)
        cols = Q_GROUP * BLOCK
        key = lax.broadcasted_iota(jnp.int32, (BLOCK, cols), 0)
        r = lax.broadcasted_iota(jnp.int32, (BLOCK, cols), 1) % BLOCK
    else:
        k_all, vt_all = ka_ref[...].astype(BF16), va_ref[...].T.astype(BF16)
    for qb in range(TB // BLOCK):
        q0 = qb * BLOCK
        bias_fn = None
        if latent:
            k_all = jnp.concatenate([k_seq[q0:q0 + 3 * BLOCK], k_ctx], axis=0)
            vt_all = jnp.concatenate([vt_seq[:, q0:q0 + 3 * BLOCK], vt_ctx], axis=1)
            blk = i * (TB // BLOCK) + qb
            b_prev = jnp.where((key >= r) & (blk > 0), 0.0, NEG_INF).astype(F32)
            b_next = jnp.where((key <= r) & (blk < n // BLOCK - 1), 0.0, NEG_INF).astype(F32)

            def bias_fn(s, b_prev=b_prev, b_next=b_next):
                return jnp.concatenate(
                    [s[0:BLOCK] + b_prev, s[BLOCK:2 * BLOCK],
                     s[2 * BLOCK:3 * BLOCK] + b_next, s[3 * BLOCK:]], axis=0)

        attn = _attend(q_ref, q0, k_all, vt_all, sink_ref, layer, bias_fn)
        o_ref[q0:q0 + BLOCK, POOL_WIDTH + CONV_WIDTH:] = attn.astype(BF16)


def _mixer(pool_in, glu, q, k, v, wts, layer, cache=None):
    b, n, _ = pool_in.shape
    nb = n // TB
    hpb = TB // HALO
    nh = n // HALO
    kpb = TB // BLOCK
    nk = n // BLOCK

    def cur(width):
        return pl.BlockSpec((None, TB, width), lambda bi, i: (bi, i, 0))

    def halo_prev(width):
        return pl.BlockSpec((None, HALO, width), lambda bi, i: (bi, jnp.maximum(i * hpb - 1, 0), 0))

    def halo_next(width):
        return pl.BlockSpec((None, HALO, width),
                            lambda bi, i: (bi, jnp.minimum((i + 1) * hpb, nh - 1), 0))

    def const(shape, index):
        return pl.BlockSpec(shape, lambda bi, i: index)

    in_specs = [halo_prev(POOL_WIDTH), cur(POOL_WIDTH), halo_next(POOL_WIDTH),
                halo_prev(CONV_WIDTH), cur(CONV_WIDTH), halo_next(CONV_WIDTH),
                cur(ATTN_WIDTH)]
    args = [pool_in, pool_in, pool_in, glu, glu, glu, q]
    if cache is not None:
        cache_k, cache_v = cache
        past = cache_k.shape[2]
        blk_prev = pl.BlockSpec((None, BLOCK, KV_WIDTH),
                                lambda bi, i: (bi, jnp.maximum(i * kpb - 1, 0), 0))
        blk_next = pl.BlockSpec((None, BLOCK, KV_WIDTH),
                                lambda bi, i: (bi, jnp.minimum((i + 1) * kpb, nk - 1), 0))
        cspec = pl.BlockSpec((None, None, past, KV_WIDTH), lambda bi, i: (bi, layer, 0, 0))
        in_specs += [blk_prev, cur(KV_WIDTH), blk_next, blk_prev, cur(KV_WIDTH), blk_next,
                     cspec, cspec]
        args += [k, k, k, v, v, v, cache_k, cache_v]
    else:
        whole = pl.BlockSpec((None, n, KV_WIDTH), lambda bi, i: (bi, 0, 0))
        in_specs += [whole, whole]
        args += [k, v]
    in_specs += [
        const((None, POOL_WIDTH, POOL_WIDTH), (layer, 0, 0)),
        const((None, 1, POOL_WIDTH), (layer, 0, 0)),
        const((None, CONV_TAPS, CONV_WIDTH), (layer, 0, 0)),
        const((None, 1, CONV_WIDTH), (layer, 0, 0)),
        const((None, 1, CONV_WIDTH), (layer, 0, 0)),
        const((None, CONV_WIDTH, CONV_WIDTH), (layer, 0, 0)),
        pl.BlockSpec(memory_space=pltpu.SMEM),
    ]
    args += [wts["pool_bd"], wts["pool_scale"], wts["conv_dw"], wts["conv_b"],
             wts["conv_norm_g"], wts["conv_pw"], wts["sink"]]
    return pl.pallas_call(
        functools.partial(_mixer_kernel, layer=layer, latent=cache is not None, n=n),
        grid=(b, nb),
        in_specs=in_specs,
        out_specs=pl.BlockSpec((None, TB, MIX_WIDTH), lambda bi, i: (bi, i, 0)),
        out_shape=jax.ShapeDtypeStruct((b, n, MIX_WIDTH), BF16),
        scratch_shapes=[pltpu.VMEM((TB + 2 * HALO, POOL_WIDTH), F32),
                        pltpu.VMEM((TB + 2 * HALO, CONV_WIDTH), F32),
                        pltpu.VMEM((8, ZROWS, POOL_WIDTH), F32),
                        pltpu.VMEM((8, ZROWS, CONV_WIDTH), F32),
                        pltpu.VMEM((TB, POOL_WIDTH), BF16),
                        pltpu.VMEM((TB, CONV_WIDTH), BF16)],
        compiler_params=pltpu.CompilerParams(
            dimension_semantics=("arbitrary", "arbitrary"), vmem_limit_bytes=VMEM_LIMIT),
        name="mixer_latent" if cache is not None else "mixer_context",
    )(*args)


def _rope_tables(n):
    t = jnp.arange(n)
    row = (t // GRID_W).astype(F32)
    col = (t % GRID_W).astype(F32)
    half = HEAD_DIM // 2
    inv = ROPE_THETA ** (-jnp.arange(0, half, 2, dtype=F32) / half)
    zeros = jnp.zeros((n, half // 2), F32)
    cos, s_up, s_dn = [], [], []
    for pos in (row, col):
        ang = pos[:, None] * inv[None, :]
        cos += [jnp.cos(ang), jnp.cos(ang)]
        s_up += [-jnp.sin(ang), zeros]
        s_dn += [zeros, jnp.sin(ang)]
    return tuple(jnp.concatenate(parts * N_KV_HEADS, axis=1) for parts in (cos, s_up, s_dn))


def _block_diag_mean(width):
    idx = jnp.arange(width) // HEAD_DIM
    return jnp.where(idx[:, None] == idx[None, :], 1.0 / HEAD_DIM, 0.0).astype(BF16)


def kernel(x_prompt, x_sample, cache_k, cache_v, c, c_ctx, mod_w, mod_b, norm_g, ffn1_wi, ffn1_wo, ffn2_wi, ffn2_wo, w_in, w_out, pool_w, pool_scale, conv_dw, conv_b, conv_norm_g, conv_pw, q_norm_g, k_norm_g, sink):
    batch, seq, _ = x_prompt.shape
    dec_batch, dec_seq, _ = x_sample.shape
    past = cache_k.shape[2]
    assert 1 + dec_batch <= COND_ROWS

    cond = jnp.zeros((COND_ROWS, D), F32).at[0].set(c_ctx).at[1:1 + dec_batch].set(c)
    mods = _modulations(cond, mod_w, mod_b).reshape(DEPTH, COND_ROWS, 1, N_MOD * D)

    norm_g4 = norm_g.reshape(DEPTH, 3, 1, D)
    wi1, wo1 = ffn1_wi.astype(BF16), ffn1_wo.astype(BF16)
    wi2, wo2 = ffn2_wi.astype(BF16), ffn2_wo.astype(BF16)
    w_in_b, w_out_b = w_in.astype(BF16), w_out.astype(BF16)
    eye = jnp.eye(POOL_GROUPS, dtype=F32)
    pool_bd = (eye[None, :, None, :, None] * pool_w[:, :, :, None, :]).reshape(
        DEPTH, POOL_WIDTH, POOL_WIDTH).astype(BF16)
    wts = {
        "pool_bd": pool_bd,
        "pool_scale": pool_scale.reshape(DEPTH, 1, POOL_WIDTH),
        "conv_dw": conv_dw,
        "conv_b": conv_b.reshape(DEPTH, 1, CONV_WIDTH),
        "conv_norm_g": conv_norm_g.reshape(DEPTH, 1, CONV_WIDTH),
        "conv_pw": conv_pw.astype(BF16),
        "sink": sink.reshape(DEPTH * N_HEADS),
    }
    mq, mk = _block_diag_mean(ATTN_WIDTH), _block_diag_mean(KV_WIDTH)
    qg = jnp.tile(q_norm_g, (1, N_HEADS)).reshape(DEPTH, 1, ATTN_WIDTH)
    kg = jnp.tile(k_norm_g, (1, N_KV_HEADS)).reshape(DEPTH, 1, KV_WIDTH)
    rope = _rope_tables(dec_seq)
    cache_k4 = cache_k.reshape(dec_batch, DEPTH, past, KV_WIDTH)
    cache_v4 = cache_v.reshape(dec_batch, DEPTH, past, KV_WIDTH)

    def layer_fn(x, layer, bsz, n, tokens_per_cond, first_row, latent):
        x = _ffn(x, mods, norm_g4, wi1, wo1, layer, 0, tokens_per_cond, first_row)
        pool_in, glu, q, k, v = _proj(
            x, mods, norm_g4, w_in_b, mq, mk, qg, kg, layer, tokens_per_cond, first_row,
            rope_tables=rope if latent else None)

        def seqs(a):
            return a.reshape(bsz, n, a.shape[-1])

        mix = _mixer(seqs(pool_in), seqs(glu), seqs(q), seqs(k), seqs(v), wts, layer,
                     cache=(cache_k4, cache_v4) if latent else None)
        x = _ffn(x, mods, norm_g4, wi2, wo2, layer, 2, tokens_per_cond, first_row,
                 mix=mix.reshape(bsz * n, MIX_WIDTH), w_out=w_out_b)
        return x, k, v

    yp = x_prompt.reshape(batch * seq, D)
    ys = x_sample.reshape(dec_batch * dec_seq, D)
    ks, vs = [], []
    for layer in range(DEPTH):
        yp, k_l, v_l = layer_fn(yp, layer, batch, seq, None, 0, False)
        ks.append(k_l.reshape(batch, seq, N_KV_HEADS, HEAD_DIM))
        vs.append(v_l.reshape(batch, seq, N_KV_HEADS, HEAD_DIM))
        ys, _, _ = layer_fn(ys, layer, dec_batch, dec_seq, dec_seq, 1, True)
    return (yp.reshape(batch, seq, D), ys.reshape(dec_batch, dec_seq, D),
            jnp.stack(ks, axis=1), jnp.stack(vs, axis=1))
```

```python
import functools

import jax
import jax.numpy as jnp
import numpy as np
from jax import lax
from jax.experimental import pallas as pl
from jax.experimental.pallas import tpu as pltpu

D = 1024
DEPTH = 2
GRID_W = 64
POOL_WIDTH = 256
POOL_GROUPS = 4
POOL_GROUP_DIM = POOL_WIDTH // POOL_GROUPS
CONV_WIDTH = 256
CONV_TAPS = 31
N_HEADS = 8
N_KV_HEADS = 2
HEAD_DIM = 64
Q_GROUP = N_HEADS // N_KV_HEADS
ATTN_WIDTH = N_HEADS * HEAD_DIM
KV_WIDTH = N_KV_HEADS * HEAD_DIM
MIX_WIDTH = POOL_WIDTH + CONV_WIDTH + ATTN_WIDTH
ATTN_OFFSET = POOL_WIDTH + 2 * CONV_WIDTH
IN_WIDTH = ATTN_OFFSET + ATTN_WIDTH + 2 * KV_WIDTH
BLOCK = 128
D_FF = 2816
N_MOD = 9
ROPE_THETA = 10000.0
EPS = 1e-6
NEG_INF = -1e30
SCALE = HEAD_DIM ** -0.5
LOG2E = 1.4426950408889634

COND_ROWS = 8
HALO = 16
TM_FFN = 1024
FFN_SUB = 512
FFN_PIECE = 128
TM_PROJ = 512
MXU_TILE = 256
FFN_CHUNK = 2 * MXU_TILE
TB = 256
ROW_CHUNK = 64
ZROWS = TB + 24
VMEM_LIMIT = 56 * 1024 * 1024

F32 = jnp.float32
BF16 = jnp.bfloat16


def _dot(a, b):
    return jnp.dot(a, b, preferred_element_type=F32)


def _rms_mod(x, g, sc, sh):
    ms = jnp.mean(x * x, axis=-1, keepdims=True)
    return (x * lax.rsqrt(ms + EPS) * g) * (1.0 + sc) + sh


def _mod_kernel(cond_ref, w_ref, b_ref, o_ref):
    c = cond_ref[...]
    s = (c * jax.nn.sigmoid(c)).astype(BF16)
    o_ref[...] = _dot(s, w_ref[...].astype(BF16)) + b_ref[...]


def _modulations(cond, mod_w, mod_b):
    tn = 1024
    nt = (N_MOD * D) // tn
    return pl.pallas_call(
        _mod_kernel,
        grid=(DEPTH, nt),
        in_specs=[
            pl.BlockSpec((COND_ROWS, D), lambda l, j: (0, 0)),
            pl.BlockSpec((None, D, tn), lambda l, j: (l, 0, j)),
            pl.BlockSpec((None, 1, tn), lambda l, j: (l, 0, j)),
        ],
        out_specs=pl.BlockSpec((None, COND_ROWS, tn), lambda l, j: (l, 0, j)),
        out_shape=jax.ShapeDtypeStruct((DEPTH, COND_ROWS, N_MOD * D), F32),
        compiler_params=pltpu.CompilerParams(
            dimension_semantics=("arbitrary", "arbitrary"),
            vmem_limit_bytes=VMEM_LIMIT),
        name="modulations",
    )(cond, mod_w, mod_b.reshape(DEPTH, 1, N_MOD * D))


def _cond_row(i, tm, tokens_per_cond, first_row):
    if tokens_per_cond is None:
        return first_row
    return first_row + i // (tokens_per_cond // tm)


def _mod_spec(layer, which, tm, tokens_per_cond, first_row):
    return pl.BlockSpec(
        (None, None, 1, 3 * D),
        lambda i: (layer, _cond_row(i, tm, tokens_per_cond, first_row), 0, which))


def _const_spec(shape, index):
    return pl.BlockSpec(shape, lambda i: index, pipeline_mode=pl.Buffered(1))


def _ffn_kernel(*refs, pre_outproj):
    if pre_outproj:
        (x_ref, mix_ref, wout_ref, modp_ref, mod_ref, g_ref, wi_ref, wo_ref,
         o_ref, a_ref, h_ref) = refs
        x1_ref = o_ref
    else:
        x_ref, mod_ref, g_ref, wi_ref, wo_ref, o_ref, a_ref, h_ref = refs
        x1_ref = x_ref
    n_sub = x_ref.shape[0] // FFN_SUB

    def rows(s):
        return slice(s * FFN_SUB, (s + 1) * FFN_SUB)

    def outproj(s):
        if pre_outproj:
            x1_ref[rows(s), :] = x_ref[rows(s), :] + modp_ref[:, 2 * D:3 * D] * _dot(
                mix_ref[rows(s), :], wout_ref[...])

    def prologue(s, piece):
        r0 = s * FFN_SUB + piece * FFN_PIECE
        p0 = piece * FFN_PIECE
        x = x1_ref[r0:r0 + FFN_PIECE, :]
        h = _rms_mod(x, g_ref[...], mod_ref[:, D:2 * D], mod_ref[:, 0:D]).astype(BF16)
        h_ref[s, p0:p0 + FFN_PIECE, :] = h
        token = h[0:16, :]
        for r in range(16, FFN_PIECE, 16):
            token = token + h[r:r + 16, :]
        return functools.reduce(lambda a, b: a + b, [token[:, l:l + 128] for l in range(0, D, 128)])

    def anchor(s, token):
        zero = jnp.zeros_like(token)
        h_ref[s, 0:16, 0:128] = h_ref[s, 0:16, 0:128] + jnp.maximum(jnp.minimum(token, zero), zero)

    n_piece = FFN_SUB // FFN_PIECE
    outproj(0)
    for piece in range(n_piece):
        prologue(0, piece)
    for s in range(n_sub):
        if s + 1 < n_sub:
            outproj(s + 1)
        for c, c0 in enumerate(range(0, D_FF, FFN_CHUNK)):
            c1 = min(c0 + FFN_CHUNK, D_FF)
            gate = _dot(h_ref[s], wi_ref[:, c0:c1])
            up = _dot(h_ref[s], wi_ref[:, D_FF + c0:D_FF + c1])
            a_ref[s, :, c0:c1] = (gate * jax.nn.sigmoid(gate) * up).astype(BF16)
            if s + 1 < n_sub and c < n_piece:
                anchor(s, prologue(s + 1, c))
        o_ref[rows(s), :] = x1_ref[rows(s), :] + (0.5 * mod_ref[:, 2 * D:3 * D]) * _dot(
            a_ref[s], wo_ref[...])


def _ffn(x, mods, norm_g, wi, wo, layer, which, tokens_per_cond, first_row,
         mix=None, w_out=None):
    t = x.shape[0]
    tm = TM_FFN
    tok = pl.BlockSpec((tm, D), lambda i: (i, 0))
    in_specs = [tok]
    args = [x]
    if mix is not None:
        in_specs += [tok, _const_spec((None, MIX_WIDTH, D), (layer, 0, 0)),
                     _mod_spec(layer, 1, tm, tokens_per_cond, first_row)]
        args += [mix, w_out, mods]
    in_specs += [
        _mod_spec(layer, which, tm, tokens_per_cond, first_row),
        _const_spec((None, None, 1, D), (layer, which, 0, 0)),
        _const_spec((None, D, 2 * D_FF), (layer, 0, 0)),
        _const_spec((None, D_FF, D), (layer, 0, 0)),
    ]
    args += [mods, norm_g, wi, wo]
    return pl.pallas_call(
        functools.partial(_ffn_kernel, pre_outproj=mix is not None),
        grid=(t // tm,),
        in_specs=in_specs,
        out_specs=tok,
        out_shape=jax.ShapeDtypeStruct((t, D), F32),
        scratch_shapes=[pltpu.VMEM((tm // FFN_SUB, FFN_SUB, D_FF), BF16),
                        pltpu.VMEM((tm // FFN_SUB, FFN_SUB, D), BF16)],
        compiler_params=pltpu.CompilerParams(
            dimension_semantics=("arbitrary",), vmem_limit_bytes=VMEM_LIMIT),
        name="ffn_outproj" if mix is not None else "ffn",
    )(*args)


def _head_norm(x, m, g):
    sq = x * x
    hi = sq.astype(BF16)
    lo = (sq - hi.astype(F32)).astype(BF16)
    ms = _dot(hi, m) + _dot(lo, m)
    return x * lax.rsqrt(ms + EPS) * g


def _rope(x, c, s_up, s_dn):
    n = x.shape[1]
    return x * c + pltpu.roll(x, n - 16, 1) * s_up + pltpu.roll(x, 16, 1) * s_dn


def _proj_kernel(*refs, rope):
    if rope:
        (x_ref, mod_ref, g_ref, win_ref, mq_ref, mk_ref, qg_ref, kg_ref,
         c_ref, su_ref, sd_ref, pool_ref, glu_ref, q_ref, k_ref, v_ref) = refs
    else:
        (x_ref, mod_ref, g_ref, win_ref, mq_ref, mk_ref, qg_ref, kg_ref,
         pool_ref, glu_ref, q_ref, k_ref, v_ref) = refs
    h = _rms_mod(x_ref[...], g_ref[...], mod_ref[:, D:2 * D], mod_ref[:, 0:D]).astype(BF16)
    u = _dot(h, win_ref[...])
    pool_ref[...] = u[:, :POOL_WIDTH]
    glu_ref[...] = (u[:, POOL_WIDTH:POOL_WIDTH + CONV_WIDTH]
                    * jax.nn.sigmoid(u[:, POOL_WIDTH + CONV_WIDTH:ATTN_OFFSET]))
    q = _head_norm(u[:, ATTN_OFFSET:ATTN_OFFSET + ATTN_WIDTH], mq_ref[...], qg_ref[...])
    k = _head_norm(u[:, ATTN_OFFSET + ATTN_WIDTH:ATTN_OFFSET + ATTN_WIDTH + KV_WIDTH],
                   mk_ref[...], kg_ref[...])
    if rope:
        c, su, sd = c_ref[...], su_ref[...], sd_ref[...]
        k = _rope(k, c, su, sd)
        rep = ATTN_WIDTH // KV_WIDTH
        q = _rope(q, jnp.concatenate([c] * rep, axis=1), jnp.concatenate([su] * rep, axis=1),
                  jnp.concatenate([sd] * rep, axis=1))
    q_ref[...] = (q * (SCALE * LOG2E)).astype(BF16)
    k_ref[...] = k
    v_ref[...] = u[:, ATTN_OFFSET + ATTN_WIDTH + KV_WIDTH:]


def _proj(x, mods, norm_g, w_in, mq, mk, qg, kg, layer, tokens_per_cond, first_row,
          rope_tables=None):
    t = x.shape[0]
    tm = TM_PROJ
    in_specs = [
        pl.BlockSpec((tm, D), lambda i: (i, 0)),
        _mod_spec(layer, 1, tm, tokens_per_cond, first_row),
        _const_spec((None, None, 1, D), (layer, 1, 0, 0)),
        _const_spec((None, D, IN_WIDTH), (layer, 0, 0)),
        _const_spec((ATTN_WIDTH, ATTN_WIDTH), (0, 0)),
        _const_spec((KV_WIDTH, KV_WIDTH), (0, 0)),
        _const_spec((None, 1, ATTN_WIDTH), (layer, 0, 0)),
        _const_spec((None, 1, KV_WIDTH), (layer, 0, 0)),
    ]
    args = [x, mods, norm_g, w_in, mq, mk, qg, kg]
    if rope_tables is not None:
        seq_tiles = rope_tables[0].shape[0] // tm
        in_specs += [pl.BlockSpec((tm, KV_WIDTH), lambda i: (i % seq_tiles, 0))] * 3
        args += list(rope_tables)

    def out(width):
        return pl.BlockSpec((tm, width), lambda i: (i, 0))

    return pl.pallas_call(
        functools.partial(_proj_kernel, rope=rope_tables is not None),
        grid=(t // tm,),
        in_specs=in_specs,
        out_specs=[out(POOL_WIDTH), out(CONV_WIDTH), out(ATTN_WIDTH), out(KV_WIDTH), out(KV_WIDTH)],
        out_shape=[
            jax.ShapeDtypeStruct((t, POOL_WIDTH), F32),
            jax.ShapeDtypeStruct((t, CONV_WIDTH), F32),
            jax.ShapeDtypeStruct((t, ATTN_WIDTH), BF16),
            jax.ShapeDtypeStruct((t, KV_WIDTH), F32),
            jax.ShapeDtypeStruct((t, KV_WIDTH), F32),
        ],
        compiler_params=pltpu.CompilerParams(
            dimension_semantics=("arbitrary",), vmem_limit_bytes=VMEM_LIMIT),
        name="proj_rope" if rope_tables is not None else "proj",
    )(*args)


def _fill_ext(ext_ref, prev_ref, cur_ref, next_ref, i, nb):
    ext_ref[0:HALO, :] = jnp.where(i > 0, prev_ref[...], 0.0)
    ext_ref[HALO:HALO + TB, :] = cur_ref[...]
    ext_ref[HALO + TB:, :] = jnp.where(i < nb - 1, next_ref[...], 0.0)


def _shift_copies(ext_ref, z_ref, shifts, lanes):
    for b in shifts:
        z_ref[b, :, lanes] = ext_ref[b:b + ZROWS, lanes]


def _window(z_ref, ext_off, r0, lanes):
    a, b = divmod(ext_off, 8)
    return z_ref[b, 8 * a + r0:8 * a + r0 + ROW_CHUNK, lanes]


def _pool_chunk(z_ref, pooled_ref, r0, i, n):
    lo_lanes, hi_lanes = slice(0, 128), slice(128, 256)
    first = lax.broadcasted_iota(jnp.int32, (ROW_CHUNK, 128), 1) < POOL_GROUP_DIM
    t = i * TB + r0 + lax.broadcasted_iota(jnp.int32, (ROW_CHUNK, 128), 0)

    def centred(total, half, cur):
        count = jnp.minimum(t + half, n) - jnp.maximum(t - half, 0)
        return (total / count.astype(F32) - cur).astype(BF16)

    def u(off, lanes):
        return _window(z_ref, HALO + off, r0, lanes)

    cur = u(0, lo_lanes)
    acc2 = u(-1, lo_lanes) + cur
    acc4 = acc2 + u(-2, lo_lanes) + u(1, lo_lanes)
    pooled_ref[r0:r0 + ROW_CHUNK, lo_lanes] = centred(
        jnp.where(first, acc2, acc4), jnp.where(first, 1, 2), cur)
    cur = u(0, hi_lanes)
    acc8 = cur
    for off in (-4, -3, -2, -1, 1, 2, 3):
        acc8 = acc8 + u(off, hi_lanes)
    acc16 = acc8
    for off in (-8, -7, -6, -5, 4, 5, 6, 7):
        acc16 = acc16 + u(off, hi_lanes)
    pooled_ref[r0:r0 + ROW_CHUNK, hi_lanes] = centred(
        jnp.where(first, acc8, acc16), jnp.where(first, 4, 8), cur)


def _conv_chunk(z_ref, act_ref, r0, dw_ref, cb_ref, cg_ref):
    acc = None
    for k in range(CONV_TAPS):
        term = _window(z_ref, HALO - CONV_TAPS // 2 + k, r0, slice(None)) * dw_ref[k:k + 1, :]
        acc = term if acc is None else acc + term
    y = acc + cb_ref[...]
    ms = jnp.mean(y * y, axis=-1, keepdims=True)
    z = y * lax.rsqrt(ms + EPS) * cg_ref[...]
    act_ref[r0:r0 + ROW_CHUNK, :] = (z * jax.nn.sigmoid(z)).astype(BF16)


def _att_scores(s_ref, q_ref, q0, j, kb, biases):
    lane = lax.broadcasted_iota(jnp.int32, (BLOCK, KV_WIDTH), 1)
    own = (lane // HEAD_DIM) == j
    qs = []
    for h in range(j * Q_GROUP, (j + 1) * Q_GROUP):
        c0 = (h // 2) * KV_WIDTH
        qh = q_ref[q0:q0 + BLOCK, c0:c0 + KV_WIDTH]
        if h % 2 != j:
            qh = jnp.concatenate([qh[:, HEAD_DIM:], qh[:, :HEAD_DIM]], axis=1)
        qs.append(jnp.where(own, qh, jnp.zeros_like(qh)))
    qj = jnp.concatenate(qs, axis=0)
    s = lax.dot_general(kb, qj, (((1,), (1,)), ((), ())), preferred_element_type=F32)
    if biases is None:
        s_ref[...] = s
    else:
        s_ref[0:BLOCK] = s[0:BLOCK] + biases[0]
        s_ref[BLOCK:2 * BLOCK] = s[BLOCK:2 * BLOCK]
        s_ref[2 * BLOCK:3 * BLOCK] = s[2 * BLOCK:3 * BLOCK] + biases[1]
        s_ref[3 * BLOCK:] = s[3 * BLOCK:]


def _att_weights(p_ref, s_ref, j, sink_ref, layer):
    sink = jnp.concatenate(
        [jnp.full((1, BLOCK), sink_ref[layer * N_HEADS + j * Q_GROUP + g] * LOG2E, F32)
         for g in range(Q_GROUP)], axis=1)
    m = jnp.maximum(jnp.max(s_ref[...], axis=0, keepdims=True), sink)
    for r in range(0, s_ref.shape[0], BLOCK):
        p_ref[r:r + BLOCK] = jnp.exp2(s_ref[r:r + BLOCK] - m).astype(BF16)
    return jnp.exp2(sink - m)


def _att_values(p_ref, sink_w, j, vt):
    v_ones = jnp.concatenate(
        [vt[j * HEAD_DIM:(j + 1) * HEAD_DIM, :], jnp.ones((16, vt.shape[1]), BF16)], axis=0)
    o = _dot(v_ones, p_ref[...])
    return o[0:HEAD_DIM, :] / (o[HEAD_DIM:HEAD_DIM + 1, :] + sink_w)


def _mixer_kernel(*refs, layer, latent, n):
    nb = n // TB
    i = pl.program_id(1)
    (pp_ref, pc_ref, pn_ref, gp_ref, gc_ref, gn_ref, q_ref) = refs[:7]
    rest = refs[7:]
    if latent:
        (kp_ref, kc_ref, kn_ref, vp_ref, vc_ref, vn_ref, ck_ref, cv_ref) = rest[:8]
        rest = rest[8:]
    else:
        (ka_ref, va_ref) = rest[:2]
        rest = rest[2:]
    (poolw_ref, pscale_ref, dw_ref, cb_ref, cg_ref, pw_ref, sink_ref,
     o_ref, pext_ref, gext_ref, zp_ref, zg_ref, pooled_ref, act_ref,
     s_ref, p_ref, at_ref) = rest

    _fill_ext(pext_ref, pp_ref, pc_ref, pn_ref, i, nb)
    _shift_copies(pext_ref, zp_ref, (6, 7, 0, 1), slice(0, 128))
    _shift_copies(pext_ref, zp_ref, range(8), slice(128, 256))
    _fill_ext(gext_ref, gp_ref, gc_ref, gn_ref, i, nb)
    _shift_copies(gext_ref, zg_ref, range(8), slice(None))
    vpu_chunks = []
    for r0 in range(0, TB, ROW_CHUNK):
        vpu_chunks.append(functools.partial(_pool_chunk, zp_ref, pooled_ref, r0, i, n))
        vpu_chunks.append(functools.partial(_conv_chunk, zg_ref, act_ref, r0, dw_ref, cb_ref, cg_ref))

    if latent:
        k_seq = jnp.concatenate([kp_ref[...], kc_ref[...], kn_ref[...]], axis=0).astype(BF16)
        vt_seq = jnp.concatenate([vp_ref[...], vc_ref[...], vn_ref[...]], axis=0).T.astype(BF16)
        k_ctx, vt_ctx = ck_ref[...].astype(BF16), cv_ref[...].T.astype(BF16)
        cols = Q_GROUP * BLOCK
        key = lax.broadcasted_iota(jnp.int32, (BLOCK, cols), 0)
        r = lax.broadcasted_iota(jnp.int32, (BLOCK, cols), 1) % BLOCK
    else:
        k_all, vt_all = ka_ref[...].astype(BF16), va_ref[...].T.astype(BF16)
    operands = []
    for qb in range(TB // BLOCK):
        q0 = qb * BLOCK
        biases = None
        if latent:
            k_all = jnp.concatenate([k_seq[q0:q0 + 3 * BLOCK], k_ctx], axis=0)
            vt_all = jnp.concatenate([vt_seq[:, q0:q0 + 3 * BLOCK], vt_ctx], axis=1)
            blk = i * (TB // BLOCK) + qb
            biases = (jnp.where((key >= r) & (blk > 0), 0.0, NEG_INF).astype(F32),
                      jnp.where((key <= r) & (blk < n // BLOCK - 1), 0.0, NEG_INF).astype(F32))
        operands.append((q0, k_all, vt_all, biases))

    units = [(qb, j) for qb in range(TB // BLOCK) for j in range(N_KV_HEADS)]
    n_stage = len(units) + 2
    per_stage = -(-len(vpu_chunks) // n_stage)

    def pool_conv_matmuls():
        o_ref[:, 0:POOL_WIDTH] = (
            _dot(pooled_ref[...], poolw_ref[...]) * pscale_ref[...]).astype(BF16)
        o_ref[:, POOL_WIDTH:POOL_WIDTH + CONV_WIDTH] = _dot(act_ref[...], pw_ref[...]).astype(BF16)

    if latent:
        for chunk in vpu_chunks:
            chunk()
        vpu_chunks = []
        pool_conv_matmuls()
    sink_w = {}
    for t in range(n_stage):
        if t < len(units):
            qb, j = units[t]
            q0, k_all, _, biases = operands[qb]
            _att_scores(s_ref.at[t % 2], q_ref, q0, j, k_all, biases)
        if 0 <= t - 1 < len(units):
            u = t - 1
            sink_w[u] = _att_weights(p_ref.at[u % 2], s_ref.at[u % 2], units[u][1], sink_ref, layer)
        if 0 <= t - 2 < len(units):
            u = t - 2
            qb, j = units[u]
            o = _att_values(p_ref.at[u % 2], sink_w.pop(u), j, operands[qb][2])
            for g in range(Q_GROUP):
                c0 = (j * Q_GROUP + g) * HEAD_DIM
                at_ref[qb, c0:c0 + HEAD_DIM, :] = o[:, g * BLOCK:(g + 1) * BLOCK]
        for chunk in vpu_chunks[t * per_stage:(t + 1) * per_stage]:
            chunk()
    if not latent:
        pool_conv_matmuls()
    for qb in range(TB // BLOCK):
        o_ref[qb * BLOCK:(qb + 1) * BLOCK, POOL_WIDTH + CONV_WIDTH:] = at_ref[qb].T.astype(BF16)


def _mixer(pool_in, glu, q, k, v, wts, layer, cache=None):
    b, n, _ = pool_in.shape
    nb = n // TB
    hpb = TB // HALO
    nh = n // HALO
    kpb = TB // BLOCK
    nk = n // BLOCK
    n_keys = n if cache is None else 3 * BLOCK + cache[0].shape[2]

    def cur(width):
        return pl.BlockSpec((None, TB, width), lambda bi, i: (bi, i, 0))

    def halo_prev(width):
        return pl.BlockSpec((None, HALO, width), lambda bi, i: (bi, jnp.maximum(i * hpb - 1, 0), 0))

    def halo_next(width):
        return pl.BlockSpec((None, HALO, width),
                            lambda bi, i: (bi, jnp.minimum((i + 1) * hpb, nh - 1), 0))

    def const(shape, index):
        return pl.BlockSpec(shape, lambda bi, i: index)

    in_specs = [halo_prev(POOL_WIDTH), cur(POOL_WIDTH), halo_next(POOL_WIDTH),
                halo_prev(CONV_WIDTH), cur(CONV_WIDTH), halo_next(CONV_WIDTH),
                cur(ATTN_WIDTH)]
    args = [pool_in, pool_in, pool_in, glu, glu, glu, q]
    if cache is not None:
        cache_k, cache_v = cache
        past = cache_k.shape[2]
        blk_prev = pl.BlockSpec((None, BLOCK, KV_WIDTH),
                                lambda bi, i: (bi, jnp.maximum(i * kpb - 1, 0), 0))
        blk_next = pl.BlockSpec((None, BLOCK, KV_WIDTH),
                                lambda bi, i: (bi, jnp.minimum((i + 1) * kpb, nk - 1), 0))
        cspec = pl.BlockSpec((None, None, past, KV_WIDTH), lambda bi, i: (bi, layer, 0, 0))
        in_specs += [blk_prev, cur(KV_WIDTH), blk_next, blk_prev, cur(KV_WIDTH), blk_next,
                     cspec, cspec]
        args += [k, k, k, v, v, v, cache_k, cache_v]
    else:
        whole = pl.BlockSpec((None, n, KV_WIDTH), lambda bi, i: (bi, 0, 0))
        in_specs += [whole, whole]
        args += [k, v]
    in_specs += [
        const((None, POOL_WIDTH, POOL_WIDTH), (layer, 0, 0)),
        const((None, 1, POOL_WIDTH), (layer, 0, 0)),
        const((None, CONV_TAPS, CONV_WIDTH), (layer, 0, 0)),
        const((None, 1, CONV_WIDTH), (layer, 0, 0)),
        const((None, 1, CONV_WIDTH), (layer, 0, 0)),
        const((None, CONV_WIDTH, CONV_WIDTH), (layer, 0, 0)),
        pl.BlockSpec(memory_space=pltpu.SMEM),
    ]
    args += [wts["pool_bd"], wts["pool_scale"], wts["conv_dw"], wts["conv_b"],
             wts["conv_norm_g"], wts["conv_pw"], wts["sink"]]
    return pl.pallas_call(
        functools.partial(_mixer_kernel, layer=layer, latent=cache is not None, n=n),
        grid=(b, nb),
        in_specs=in_specs,
        out_specs=pl.BlockSpec((None, TB, MIX_WIDTH), lambda bi, i: (bi, i, 0)),
        out_shape=jax.ShapeDtypeStruct((b, n, MIX_WIDTH), BF16),
        scratch_shapes=[pltpu.VMEM((TB + 2 * HALO, POOL_WIDTH), F32),
                        pltpu.VMEM((TB + 2 * HALO, CONV_WIDTH), F32),
                        pltpu.VMEM((8, ZROWS, POOL_WIDTH), F32),
                        pltpu.VMEM((8, ZROWS, CONV_WIDTH), F32),
                        pltpu.VMEM((TB, POOL_WIDTH), BF16),
                        pltpu.VMEM((TB, CONV_WIDTH), BF16),
                        pltpu.VMEM((2, n_keys, Q_GROUP * BLOCK), F32),
                        pltpu.VMEM((2, n_keys, Q_GROUP * BLOCK), BF16),
                        pltpu.VMEM((TB // BLOCK, ATTN_WIDTH, BLOCK), F32)],
        compiler_params=pltpu.CompilerParams(
            dimension_semantics=("arbitrary", "arbitrary"), vmem_limit_bytes=VMEM_LIMIT),
        name="mixer_latent" if cache is not None else "mixer_context",
    )(*args)


def _rope_tables(n):
    t = np.arange(n)
    half = HEAD_DIM // 2
    inv = ROPE_THETA ** (-np.arange(0, half, 2, dtype=np.float64) / half)
    zeros = np.zeros((n, half // 2))
    cos, s_up, s_dn = [], [], []
    for pos in (t // GRID_W, t % GRID_W):
        ang = pos[:, None].astype(np.float64) * inv[None, :]
        cos += [np.cos(ang), np.cos(ang)]
        s_up += [-np.sin(ang), zeros]
        s_dn += [zeros, np.sin(ang)]
    return tuple(jnp.asarray(np.concatenate(parts * N_KV_HEADS, axis=1), dtype=F32)
                 for parts in (cos, s_up, s_dn))


def _block_diag_mean(width):
    idx = jnp.arange(width) // HEAD_DIM
    return jnp.where(idx[:, None] == idx[None, :], 1.0 / HEAD_DIM, 0.0).astype(BF16)


def kernel(x_prompt, x_sample, cache_k, cache_v, c, c_ctx, mod_w, mod_b, norm_g, ffn1_wi, ffn1_wo, ffn2_wi, ffn2_wo, w_in, w_out, pool_w, pool_scale, conv_dw, conv_b, conv_norm_g, conv_pw, q_norm_g, k_norm_g, sink):
    batch, seq, _ = x_prompt.shape
    dec_batch, dec_seq, _ = x_sample.shape
    past = cache_k.shape[2]
    assert 1 + dec_batch <= COND_ROWS

    cond = jnp.zeros((COND_ROWS, D), F32).at[0].set(c_ctx).at[1:1 + dec_batch].set(c)
    mods = _modulations(cond, mod_w, mod_b).reshape(DEPTH, COND_ROWS, 1, N_MOD * D)

    norm_g4 = norm_g.reshape(DEPTH, 3, 1, D)
    wi1, wo1 = ffn1_wi.astype(BF16), ffn1_wo.astype(BF16)
    wi2, wo2 = ffn2_wi.astype(BF16), ffn2_wo.astype(BF16)
    w_in_b, w_out_b = w_in.astype(BF16), w_out.astype(BF16)
    eye = jnp.eye(POOL_GROUPS, dtype=F32)
    pool_bd = (eye[None, :, None, :, None] * pool_w[:, :, :, None, :]).reshape(
        DEPTH, POOL_WIDTH, POOL_WIDTH).astype(BF16)
    wts = {
        "pool_bd": pool_bd,
        "pool_scale": pool_scale.reshape(DEPTH, 1, POOL_WIDTH),
        "conv_dw": conv_dw,
        "conv_b": conv_b.reshape(DEPTH, 1, CONV_WIDTH),
        "conv_norm_g": conv_norm_g.reshape(DEPTH, 1, CONV_WIDTH),
        "conv_pw": conv_pw.astype(BF16),
        "sink": sink.reshape(DEPTH * N_HEADS),
    }
    mq, mk = _block_diag_mean(ATTN_WIDTH), _block_diag_mean(KV_WIDTH)
    qg = jnp.tile(q_norm_g, (1, N_HEADS)).reshape(DEPTH, 1, ATTN_WIDTH)
    kg = jnp.tile(k_norm_g, (1, N_KV_HEADS)).reshape(DEPTH, 1, KV_WIDTH)
    rope = _rope_tables(dec_seq)
    cache_k4 = cache_k.reshape(dec_batch, DEPTH, past, KV_WIDTH)
    cache_v4 = cache_v.reshape(dec_batch, DEPTH, past, KV_WIDTH)

    def layer_fn(x, layer, bsz, n, tokens_per_cond, first_row, latent):
        x = _ffn(x, mods, norm_g4, wi1, wo1, layer, 0, tokens_per_cond, first_row)
        pool_in, glu, q, k, v = _proj(
            x, mods, norm_g4, w_in_b, mq, mk, qg, kg, layer, tokens_per_cond, first_row,
            rope_tables=rope if latent else None)

        def seqs(a):
            return a.reshape(bsz, n, a.shape[-1])

        mix = _mixer(seqs(pool_in), seqs(glu), seqs(q), seqs(k), seqs(v), wts, layer,
                     cache=(cache_k4, cache_v4) if latent else None)
        x = _ffn(x, mods, norm_g4, wi2, wo2, layer, 2, tokens_per_cond, first_row,
                 mix=mix.reshape(bsz * n, MIX_WIDTH), w_out=w_out_b)
        return x, k, v

    yp = x_prompt.reshape(batch * seq, D)
    ys = x_sample.reshape(dec_batch * dec_seq, D)
    ks, vs = [], []
    for layer in range(DEPTH):
        yp, k_l, v_l = layer_fn(yp, layer, batch, seq, None, 0, False)
        ks.append(k_l.reshape(batch, seq, N_KV_HEADS, HEAD_DIM))
        vs.append(v_l.reshape(batch, seq, N_KV_HEADS, HEAD_DIM))
        ys, _, _ = layer_fn(ys, layer, dec_batch, dec_seq, dec_seq, 1, True)
    return (yp.reshape(batch, seq, D), ys.reshape(dec_batch, dec_seq, D),
            jnp.stack(ks, axis=1), jnp.stack(vs, axis=1))
```

```python
import functools

import jax
import jax.numpy as jnp
import numpy as np
from jax import lax
from jax.experimental import pallas as pl
from jax.experimental.pallas import tpu as pltpu

D = 1024
DEPTH = 2
GRID_W = 64
POOL_WIDTH = 256
POOL_GROUPS = 4
POOL_GROUP_DIM = POOL_WIDTH // POOL_GROUPS
CONV_WIDTH = 256
CONV_TAPS = 31
N_HEADS = 8
N_KV_HEADS = 2
HEAD_DIM = 64
Q_GROUP = N_HEADS // N_KV_HEADS
ATTN_WIDTH = N_HEADS * HEAD_DIM
KV_WIDTH = N_KV_HEADS * HEAD_DIM
MIX_WIDTH = POOL_WIDTH + CONV_WIDTH + ATTN_WIDTH
ATTN_OFFSET = POOL_WIDTH + 2 * CONV_WIDTH
IN_WIDTH = ATTN_OFFSET + ATTN_WIDTH + 2 * KV_WIDTH
BLOCK = 128
D_FF = 2816
N_MOD = 9
ROPE_THETA = 10000.0
EPS = 1e-6
NEG_INF = -1e30
SCALE = HEAD_DIM ** -0.5
LOG2E = 1.4426950408889634

COND_ROWS = 8
HALO = 16
TM_FFN = 1024
FFN_SUB = 512
FFN_PIECE = 128
TM_PROJ = 512
MXU_TILE = 256
FFN_CHUNK = 2 * MXU_TILE
TB = 256
ROW_CHUNK = 64
ZROWS = TB + 24
VMEM_LIMIT = 56 * 1024 * 1024

F32 = jnp.float32
BF16 = jnp.bfloat16


def _dot(a, b):
    return jnp.dot(a, b, preferred_element_type=F32)


def _rms_mod(x, g, sc, sh):
    ms = jnp.mean(x * x, axis=-1, keepdims=True)
    return (x * lax.rsqrt(ms + EPS) * g) * (1.0 + sc) + sh


def _mod_kernel(cond_ref, w_ref, b_ref, o_ref):
    c = cond_ref[...]
    s = (c * jax.nn.sigmoid(c)).astype(BF16)
    o_ref[...] = _dot(s, w_ref[...].astype(BF16)) + b_ref[...]


def _modulations(cond, mod_w, mod_b):
    tn = 1024
    nt = (N_MOD * D) // tn
    return pl.pallas_call(
        _mod_kernel,
        grid=(DEPTH, nt),
        in_specs=[
            pl.BlockSpec((COND_ROWS, D), lambda l, j: (0, 0)),
            pl.BlockSpec((None, D, tn), lambda l, j: (l, 0, j)),
            pl.BlockSpec((None, 1, tn), lambda l, j: (l, 0, j)),
        ],
        out_specs=pl.BlockSpec((None, COND_ROWS, tn), lambda l, j: (l, 0, j)),
        out_shape=jax.ShapeDtypeStruct((DEPTH, COND_ROWS, N_MOD * D), F32),
        compiler_params=pltpu.CompilerParams(
            dimension_semantics=("arbitrary", "arbitrary"),
            vmem_limit_bytes=VMEM_LIMIT),
        name="modulations",
    )(cond, mod_w, mod_b.reshape(DEPTH, 1, N_MOD * D))


def _cond_row(i, tm, tokens_per_cond, first_row):
    if tokens_per_cond is None:
        return first_row
    return first_row + i // (tokens_per_cond // tm)


def _mod_spec(layer, which, tm, tokens_per_cond, first_row):
    return pl.BlockSpec(
        (None, None, 1, 3 * D),
        lambda i: (layer, _cond_row(i, tm, tokens_per_cond, first_row), 0, which))


def _const_spec(shape, index):
    return pl.BlockSpec(shape, lambda i: index, pipeline_mode=pl.Buffered(1))


def _ffn_kernel(*refs, pre_outproj, side=None):
    if pre_outproj:
        (x_ref, mix_ref, wout_ref, modp_ref, mod_ref, g_ref, wi_ref, wo_ref,
         o_ref, a_ref, h_ref) = refs
        x1_ref = o_ref
    else:
        x_ref, mod_ref, g_ref, wi_ref, wo_ref, o_ref, a_ref, h_ref = refs
        x1_ref = x_ref
    n_sub = x_ref.shape[0] // FFN_SUB

    def rows(s):
        return slice(s * FFN_SUB, (s + 1) * FFN_SUB)

    def outproj(s):
        if pre_outproj:
            x1_ref[rows(s), :] = x_ref[rows(s), :] + modp_ref[:, 2 * D:3 * D] * _dot(
                mix_ref[rows(s), :], wout_ref[...])

    def prologue(s, piece):
        r0 = s * FFN_SUB + piece * FFN_PIECE
        p0 = piece * FFN_PIECE
        x = x1_ref[r0:r0 + FFN_PIECE, :]
        h = _rms_mod(x, g_ref[...], mod_ref[:, D:2 * D], mod_ref[:, 0:D]).astype(BF16)
        h_ref[s, p0:p0 + FFN_PIECE, :] = h
        token = h[0:16, :]
        for r in range(16, FFN_PIECE, 16):
            token = token + h[r:r + 16, :]
        return functools.reduce(lambda a, b: a + b, [token[:, l:l + 128] for l in range(0, D, 128)])

    def anchor(s, token):
        zero = jnp.zeros_like(token)
        h_ref[s, 0:16, 0:128] = h_ref[s, 0:16, 0:128] + jnp.maximum(jnp.minimum(token, zero), zero)

    n_piece = FFN_SUB // FFN_PIECE
    outproj(0)
    for piece in range(n_piece):
        prologue(0, piece)
    n_chunk = -(-D_FF // FFN_CHUNK)
    for s in range(n_sub):
        items = [] if side is None else side[s]
        per_chunk = -(-len(items) // n_chunk)
        if s + 1 < n_sub:
            outproj(s + 1)
        for c, c0 in enumerate(range(0, D_FF, FFN_CHUNK)):
            c1 = min(c0 + FFN_CHUNK, D_FF)
            gate = _dot(h_ref[s], wi_ref[:, c0:c1])
            up = _dot(h_ref[s], wi_ref[:, D_FF + c0:D_FF + c1])
            a_ref[s, :, c0:c1] = (gate * jax.nn.sigmoid(gate) * up).astype(BF16)
            if s + 1 < n_sub and c < n_piece:
                anchor(s, prologue(s + 1, c))
            for item in items[c * per_chunk:(c + 1) * per_chunk]:
                item()
        o_ref[rows(s), :] = x1_ref[rows(s), :] + (0.5 * mod_ref[:, 2 * D:3 * D]) * _dot(
            a_ref[s], wo_ref[...])


def _ffn(x, mods, norm_g, wi, wo, layer, which, tokens_per_cond, first_row,
         mix=None, w_out=None):
    t = x.shape[0]
    tm = TM_FFN
    tok = pl.BlockSpec((tm, D), lambda i: (i, 0))
    in_specs = [tok]
    args = [x]
    if mix is not None:
        in_specs += [tok, _const_spec((None, MIX_WIDTH, D), (layer, 0, 0)),
                     _mod_spec(layer, 1, tm, tokens_per_cond, first_row)]
        args += [mix, w_out, mods]
    in_specs += [
        _mod_spec(layer, which, tm, tokens_per_cond, first_row),
        _const_spec((None, None, 1, D), (layer, which, 0, 0)),
        _const_spec((None, D, 2 * D_FF), (layer, 0, 0)),
        _const_spec((None, D_FF, D), (layer, 0, 0)),
    ]
    args += [mods, norm_g, wi, wo]
    return pl.pallas_call(
        functools.partial(_ffn_kernel, pre_outproj=mix is not None),
        grid=(t // tm,),
        in_specs=in_specs,
        out_specs=tok,
        out_shape=jax.ShapeDtypeStruct((t, D), F32),
        scratch_shapes=[pltpu.VMEM((tm // FFN_SUB, FFN_SUB, D_FF), BF16),
                        pltpu.VMEM((tm // FFN_SUB, FFN_SUB, D), BF16)],
        compiler_params=pltpu.CompilerParams(
            dimension_semantics=("arbitrary",), vmem_limit_bytes=VMEM_LIMIT),
        name="ffn_outproj" if mix is not None else "ffn",
    )(*args)


def _head_norm(x, m, g):
    sq = x * x
    hi = sq.astype(BF16)
    lo = (sq - hi.astype(F32)).astype(BF16)
    ms = _dot(hi, m) + _dot(lo, m)
    return x * lax.rsqrt(ms + EPS) * g


def _rope(x, c, s_up, s_dn):
    n = x.shape[1]
    return x * c + pltpu.roll(x, n - 16, 1) * s_up + pltpu.roll(x, 16, 1) * s_dn


def _proj_kernel(*refs, rope):
    if rope:
        (x_ref, mod_ref, g_ref, win_ref, mq_ref, mk_ref, qg_ref, kg_ref,
         c_ref, su_ref, sd_ref, pool_ref, glu_ref, q_ref, k_ref, v_ref) = refs
    else:
        (x_ref, mod_ref, g_ref, win_ref, mq_ref, mk_ref, qg_ref, kg_ref,
         pool_ref, glu_ref, q_ref, k_ref, v_ref) = refs
    h = _rms_mod(x_ref[...], g_ref[...], mod_ref[:, D:2 * D], mod_ref[:, 0:D]).astype(BF16)
    u = _dot(h, win_ref[...])
    pool_ref[...] = u[:, :POOL_WIDTH]
    glu_ref[...] = (u[:, POOL_WIDTH:POOL_WIDTH + CONV_WIDTH]
                    * jax.nn.sigmoid(u[:, POOL_WIDTH + CONV_WIDTH:ATTN_OFFSET]))
    q = _head_norm(u[:, ATTN_OFFSET:ATTN_OFFSET + ATTN_WIDTH], mq_ref[...], qg_ref[...])
    k = _head_norm(u[:, ATTN_OFFSET + ATTN_WIDTH:ATTN_OFFSET + ATTN_WIDTH + KV_WIDTH],
                   mk_ref[...], kg_ref[...])
    if rope:
        c, su, sd = c_ref[...], su_ref[...], sd_ref[...]
        k = _rope(k, c, su, sd)
        rep = ATTN_WIDTH // KV_WIDTH
        q = _rope(q, jnp.concatenate([c] * rep, axis=1), jnp.concatenate([su] * rep, axis=1),
                  jnp.concatenate([sd] * rep, axis=1))
    q_ref[...] = (q * (SCALE * LOG2E)).astype(BF16)
    k_ref[...] = k
    v_ref[...] = u[:, ATTN_OFFSET + ATTN_WIDTH + KV_WIDTH:]


def _proj(x, mods, norm_g, w_in, mq, mk, qg, kg, layer, tokens_per_cond, first_row,
          rope_tables=None):
    t = x.shape[0]
    tm = TM_PROJ
    in_specs = [
        pl.BlockSpec((tm, D), lambda i: (i, 0)),
        _mod_spec(layer, 1, tm, tokens_per_cond, first_row),
        _const_spec((None, None, 1, D), (layer, 1, 0, 0)),
        _const_spec((None, D, IN_WIDTH), (layer, 0, 0)),
        _const_spec((ATTN_WIDTH, ATTN_WIDTH), (0, 0)),
        _const_spec((KV_WIDTH, KV_WIDTH), (0, 0)),
        _const_spec((None, 1, ATTN_WIDTH), (layer, 0, 0)),
        _const_spec((None, 1, KV_WIDTH), (layer, 0, 0)),
    ]
    args = [x, mods, norm_g, w_in, mq, mk, qg, kg]
    if rope_tables is not None:
        seq_tiles = rope_tables[0].shape[0] // tm
        in_specs += [pl.BlockSpec((tm, KV_WIDTH), lambda i: (i % seq_tiles, 0))] * 3
        args += list(rope_tables)

    def out(width):
        return pl.BlockSpec((tm, width), lambda i: (i, 0))

    return pl.pallas_call(
        functools.partial(_proj_kernel, rope=rope_tables is not None),
        grid=(t // tm,),
        in_specs=in_specs,
        out_specs=[out(POOL_WIDTH), out(CONV_WIDTH), out(ATTN_WIDTH), out(KV_WIDTH), out(KV_WIDTH)],
        out_shape=[
            jax.ShapeDtypeStruct((t, POOL_WIDTH), F32),
            jax.ShapeDtypeStruct((t, CONV_WIDTH), F32),
            jax.ShapeDtypeStruct((t, ATTN_WIDTH), BF16),
            jax.ShapeDtypeStruct((t, KV_WIDTH), F32),
            jax.ShapeDtypeStruct((t, KV_WIDTH), F32),
        ],
        compiler_params=pltpu.CompilerParams(
            dimension_semantics=("arbitrary",), vmem_limit_bytes=VMEM_LIMIT),
        name="proj_rope" if rope_tables is not None else "proj",
    )(*args)


def _fill_ext(ext_ref, prev_ref, cur_ref, next_ref, i, nb):
    ext_ref[0:HALO, :] = jnp.where(i > 0, prev_ref[...], 0.0)
    ext_ref[HALO:HALO + TB, :] = cur_ref[...]
    ext_ref[HALO + TB:, :] = jnp.where(i < nb - 1, next_ref[...], 0.0)


def _shift_copies(ext_ref, z_ref, shifts, lanes):
    for b in shifts:
        z_ref[b, :, lanes] = ext_ref[b:b + ZROWS, lanes]


def _window(z_ref, ext_off, r0, lanes):
    a, b = divmod(ext_off, 8)
    return z_ref[b, 8 * a + r0:8 * a + r0 + ROW_CHUNK, lanes]


def _pool_chunk(z_ref, pooled_ref, r0, i, n):
    lo_lanes, hi_lanes = slice(0, 128), slice(128, 256)
    first = lax.broadcasted_iota(jnp.int32, (ROW_CHUNK, 128), 1) < POOL_GROUP_DIM
    t = i * TB + r0 + lax.broadcasted_iota(jnp.int32, (ROW_CHUNK, 128), 0)

    def centred(total, half, cur):
        count = jnp.minimum(t + half, n) - jnp.maximum(t - half, 0)
        return (total / count.astype(F32) - cur).astype(BF16)

    def u(off, lanes):
        return _window(z_ref, HALO + off, r0, lanes)

    cur = u(0, lo_lanes)
    acc2 = u(-1, lo_lanes) + cur
    acc4 = acc2 + u(-2, lo_lanes) + u(1, lo_lanes)
    pooled_ref[r0:r0 + ROW_CHUNK, lo_lanes] = centred(
        jnp.where(first, acc2, acc4), jnp.where(first, 1, 2), cur)
    cur = u(0, hi_lanes)
    acc8 = cur
    for off in (-4, -3, -2, -1, 1, 2, 3):
        acc8 = acc8 + u(off, hi_lanes)
    acc16 = acc8
    for off in (-8, -7, -6, -5, 4, 5, 6, 7):
        acc16 = acc16 + u(off, hi_lanes)
    pooled_ref[r0:r0 + ROW_CHUNK, hi_lanes] = centred(
        jnp.where(first, acc8, acc16), jnp.where(first, 4, 8), cur)


def _conv_chunk(z_ref, act_ref, r0, dw_ref, cb_ref, cg_ref):
    acc = None
    for k in range(CONV_TAPS):
        term = _window(z_ref, HALO - CONV_TAPS // 2 + k, r0, slice(None)) * dw_ref[k:k + 1, :]
        acc = term if acc is None else acc + term
    y = acc + cb_ref[...]
    ms = jnp.mean(y * y, axis=-1, keepdims=True)
    z = y * lax.rsqrt(ms + EPS) * cg_ref[...]
    act_ref[r0:r0 + ROW_CHUNK, :] = (z * jax.nn.sigmoid(z)).astype(BF16)


def _att_scores(s_ref, q_ref, q0, j, kb, biases):
    lane = lax.broadcasted_iota(jnp.int32, (BLOCK, KV_WIDTH), 1)
    own = (lane // HEAD_DIM) == j
    qs = []
    for h in range(j * Q_GROUP, (j + 1) * Q_GROUP):
        c0 = (h // 2) * KV_WIDTH
        qh = q_ref[q0:q0 + BLOCK, c0:c0 + KV_WIDTH]
        if h % 2 != j:
            qh = jnp.concatenate([qh[:, HEAD_DIM:], qh[:, :HEAD_DIM]], axis=1)
        qs.append(jnp.where(own, qh, jnp.zeros_like(qh)))
    qj = jnp.concatenate(qs, axis=0)
    s = lax.dot_general(kb, qj, (((1,), (1,)), ((), ())), preferred_element_type=F32)
    if biases is None:
        s_ref[...] = s
    else:
        s_ref[0:BLOCK] = s[0:BLOCK] + biases[0]
        s_ref[BLOCK:2 * BLOCK] = s[BLOCK:2 * BLOCK]
        s_ref[2 * BLOCK:3 * BLOCK] = s[2 * BLOCK:3 * BLOCK] + biases[1]
        s_ref[3 * BLOCK:] = s[3 * BLOCK:]


def _att_weights(p_ref, s_ref, j, sink_ref, layer):
    sink = jnp.concatenate(
        [jnp.full((1, BLOCK), sink_ref[layer * N_HEADS + j * Q_GROUP + g] * LOG2E, F32)
         for g in range(Q_GROUP)], axis=1)
    m = jnp.maximum(jnp.max(s_ref[...], axis=0, keepdims=True), sink)
    for r in range(0, s_ref.shape[0], BLOCK):
        p_ref[r:r + BLOCK] = jnp.exp2(s_ref[r:r + BLOCK] - m).astype(BF16)
    return jnp.exp2(sink - m)


def _att_values(p_ref, sink_w, j, vt):
    v_ones = jnp.concatenate(
        [vt[j * HEAD_DIM:(j + 1) * HEAD_DIM, :], jnp.ones((16, vt.shape[1]), BF16)], axis=0)
    o = _dot(v_ones, p_ref[...])
    return o[0:HEAD_DIM, :] / (o[HEAD_DIM:HEAD_DIM + 1, :] + sink_w)


N_ATT_STAGES = (TB // BLOCK) * N_KV_HEADS + 2


def _mixer_parts(refs, layer, latent, n, i):
    nb = n // TB
    (pp_ref, pc_ref, pn_ref, gp_ref, gc_ref, gn_ref, q_ref) = refs[:7]
    rest = refs[7:]
    if latent:
        (kp_ref, kc_ref, kn_ref, vp_ref, vc_ref, vn_ref, ck_ref, cv_ref) = rest[:8]
        rest = rest[8:]
    else:
        (ka_ref, va_ref) = rest[:2]
        rest = rest[2:]
    (poolw_ref, pscale_ref, dw_ref, cb_ref, cg_ref, pw_ref, sink_ref,
     o_ref, pext_ref, gext_ref, zp_ref, zg_ref, pooled_ref, act_ref,
     s_ref, p_ref, at_ref) = rest

    operands = []

    def prep():
        _fill_ext(pext_ref, pp_ref, pc_ref, pn_ref, i, nb)
        _shift_copies(pext_ref, zp_ref, (6, 7, 0, 1), slice(0, 128))
        _shift_copies(pext_ref, zp_ref, range(8), slice(128, 256))
        _fill_ext(gext_ref, gp_ref, gc_ref, gn_ref, i, nb)
        _shift_copies(gext_ref, zg_ref, range(8), slice(None))
        if latent:
            k_seq = jnp.concatenate([kp_ref[...], kc_ref[...], kn_ref[...]], axis=0).astype(BF16)
            vt_seq = jnp.concatenate(
                [vp_ref[...], vc_ref[...], vn_ref[...]], axis=0).T.astype(BF16)
            k_ctx, vt_ctx = ck_ref[...].astype(BF16), cv_ref[...].T.astype(BF16)
            cols = Q_GROUP * BLOCK
            key = lax.broadcasted_iota(jnp.int32, (BLOCK, cols), 0)
            r = lax.broadcasted_iota(jnp.int32, (BLOCK, cols), 1) % BLOCK
        else:
            k_all, vt_all = ka_ref[...].astype(BF16), va_ref[...].T.astype(BF16)
        for qb in range(TB // BLOCK):
            q0 = qb * BLOCK
            biases = None
            if latent:
                k_all = jnp.concatenate([k_seq[q0:q0 + 3 * BLOCK], k_ctx], axis=0)
                vt_all = jnp.concatenate([vt_seq[:, q0:q0 + 3 * BLOCK], vt_ctx], axis=1)
                blk = i * (TB // BLOCK) + qb
                biases = (jnp.where((key >= r) & (blk > 0), 0.0, NEG_INF).astype(F32),
                          jnp.where((key <= r) & (blk < n // BLOCK - 1), 0.0, NEG_INF).astype(F32))
            operands.append((q0, k_all, vt_all, biases))

    vpu_chunks = []
    for r0 in range(0, TB, ROW_CHUNK):
        vpu_chunks.append(functools.partial(_pool_chunk, zp_ref, pooled_ref, r0, i, n))
        vpu_chunks.append(functools.partial(_conv_chunk, zg_ref, act_ref, r0, dw_ref, cb_ref, cg_ref))

    def pool_conv_matmuls():
        o_ref[:, 0:POOL_WIDTH] = (
            _dot(pooled_ref[...], poolw_ref[...]) * pscale_ref[...]).astype(BF16)
        o_ref[:, POOL_WIDTH:POOL_WIDTH + CONV_WIDTH] = _dot(act_ref[...], pw_ref[...]).astype(BF16)

    units = [(qb, j) for qb in range(TB // BLOCK) for j in range(N_KV_HEADS)]
    sink_w = {}

    def att_stage(t):
        if t < len(units):
            qb, j = units[t]
            q0, k_all, _, biases = operands[qb]
            _att_scores(s_ref.at[t % 2], q_ref, q0, j, k_all, biases)
        if 0 <= t - 1 < len(units):
            u = t - 1
            sink_w[u] = _att_weights(p_ref.at[u % 2], s_ref.at[u % 2], units[u][1], sink_ref, layer)
        if 0 <= t - 2 < len(units):
            u = t - 2
            qb, j = units[u]
            o = _att_values(p_ref.at[u % 2], sink_w.pop(u), j, operands[qb][2])
            for g in range(Q_GROUP):
                c0 = (j * Q_GROUP + g) * HEAD_DIM
                at_ref[qb, c0:c0 + HEAD_DIM, :] = o[:, g * BLOCK:(g + 1) * BLOCK]

    def att_store():
        for qb in range(TB // BLOCK):
            o_ref[qb * BLOCK:(qb + 1) * BLOCK, POOL_WIDTH + CONV_WIDTH:] = (
                at_ref[qb].T.astype(BF16))

    return prep, att_stage, vpu_chunks, pool_conv_matmuls, att_store


def _mixer_items(parts):
    prep, att_stage, vpu_chunks, pool_conv_matmuls, att_store = parts
    per_stage = -(-len(vpu_chunks) // N_ATT_STAGES)

    def stage(t):
        att_stage(t)
        for chunk in vpu_chunks[t * per_stage:(t + 1) * per_stage]:
            chunk()

    def finish():
        pool_conv_matmuls()
        att_store()

    return [prep] + [functools.partial(stage, t) for t in range(N_ATT_STAGES)] + [finish]


def _mixer_kernel(*refs, layer, latent, n):
    prep, att_stage, vpu_chunks, pool_conv_matmuls, att_store = _mixer_parts(
        refs, layer, latent, n, pl.program_id(1))
    prep()
    if latent:
        for chunk in vpu_chunks:
            chunk()
        pool_conv_matmuls()
        for t in range(N_ATT_STAGES):
            att_stage(t)
    else:
        per_stage = -(-len(vpu_chunks) // N_ATT_STAGES)
        for t in range(N_ATT_STAGES):
            att_stage(t)
            for chunk in vpu_chunks[t * per_stage:(t + 1) * per_stage]:
                chunk()
        pool_conv_matmuls()
    att_store()


def _mixer(pool_in, glu, q, k, v, wts, layer, cache=None):
    b, n, _ = pool_in.shape
    nb = n // TB
    hpb = TB // HALO
    nh = n // HALO
    kpb = TB // BLOCK
    nk = n // BLOCK
    n_keys = n if cache is None else 3 * BLOCK + cache[0].shape[2]

    def cur(width):
        return pl.BlockSpec((None, TB, width), lambda bi, i: (bi, i, 0))

    def halo_prev(width):
        return pl.BlockSpec((None, HALO, width), lambda bi, i: (bi, jnp.maximum(i * hpb - 1, 0), 0))

    def halo_next(width):
        return pl.BlockSpec((None, HALO, width),
                            lambda bi, i: (bi, jnp.minimum((i + 1) * hpb, nh - 1), 0))

    in_specs = [halo_prev(POOL_WIDTH), cur(POOL_WIDTH), halo_next(POOL_WIDTH),
                halo_prev(CONV_WIDTH), cur(CONV_WIDTH), halo_next(CONV_WIDTH),
                cur(ATTN_WIDTH)]
    args = [pool_in, pool_in, pool_in, glu, glu, glu, q]
    if cache is not None:
        cache_k, cache_v = cache
        past = cache_k.shape[2]
        blk_prev = pl.BlockSpec((None, BLOCK, KV_WIDTH),
                                lambda bi, i: (bi, jnp.maximum(i * kpb - 1, 0), 0))
        blk_next = pl.BlockSpec((None, BLOCK, KV_WIDTH),
                                lambda bi, i: (bi, jnp.minimum((i + 1) * kpb, nk - 1), 0))
        cspec = pl.BlockSpec((None, None, past, KV_WIDTH), lambda bi, i: (bi, layer, 0, 0))
        in_specs += [blk_prev, cur(KV_WIDTH), blk_next, blk_prev, cur(KV_WIDTH), blk_next,
                     cspec, cspec]
        args += [k, k, k, v, v, v, cache_k, cache_v]
    else:
        whole = pl.BlockSpec((None, n, KV_WIDTH), lambda bi, i: (bi, 0, 0))
        in_specs += [whole, whole]
        args += [k, v]
    in_specs += _mixer_weight_specs(layer)
    args += _mixer_weight_args(wts)
    return pl.pallas_call(
        functools.partial(_mixer_kernel, layer=layer, latent=cache is not None, n=n),
        grid=(b, nb),
        in_specs=in_specs,
        out_specs=pl.BlockSpec((None, TB, MIX_WIDTH), lambda bi, i: (bi, i, 0)),
        out_shape=jax.ShapeDtypeStruct((b, n, MIX_WIDTH), BF16),
        scratch_shapes=_mixer_scratch(n_keys),
        compiler_params=pltpu.CompilerParams(
            dimension_semantics=("arbitrary", "arbitrary"), vmem_limit_bytes=VMEM_LIMIT),
        name="mixer_latent" if cache is not None else "mixer_context",
    )(*args)


def _mixer_weight_specs(layer):
    def const(shape, index):
        return pl.BlockSpec(shape, lambda *_: index)

    return [
        const((None, POOL_WIDTH, POOL_WIDTH), (layer, 0, 0)),
        const((None, 1, POOL_WIDTH), (layer, 0, 0)),
        const((None, CONV_TAPS, CONV_WIDTH), (layer, 0, 0)),
        const((None, 1, CONV_WIDTH), (layer, 0, 0)),
        const((None, 1, CONV_WIDTH), (layer, 0, 0)),
        const((None, CONV_WIDTH, CONV_WIDTH), (layer, 0, 0)),
        pl.BlockSpec(memory_space=pltpu.SMEM),
    ]


def _mixer_weight_args(wts):
    return [wts["pool_bd"], wts["pool_scale"], wts["conv_dw"], wts["conv_b"],
            wts["conv_norm_g"], wts["conv_pw"], wts["sink"]]


def _mixer_scratch(n_keys):
    return [pltpu.VMEM((TB + 2 * HALO, POOL_WIDTH), F32),
            pltpu.VMEM((TB + 2 * HALO, CONV_WIDTH), F32),
            pltpu.VMEM((8, ZROWS, POOL_WIDTH), F32),
            pltpu.VMEM((8, ZROWS, CONV_WIDTH), F32),
            pltpu.VMEM((TB, POOL_WIDTH), BF16),
            pltpu.VMEM((TB, CONV_WIDTH), BF16),
            pltpu.VMEM((2, n_keys, Q_GROUP * BLOCK), F32),
            pltpu.VMEM((2, n_keys, Q_GROUP * BLOCK), BF16),
            pltpu.VMEM((TB // BLOCK, ATTN_WIDTH, BLOCK), F32)]


N_CTX_BLOCK_OPERANDS = 9


def _ffn_side_kernel(*refs, n_ffn_in, n_mix_in, n_ffn_scratch, layer, n):
    ffn_in = refs[:n_ffn_in]
    mix_in = refs[n_ffn_in:n_ffn_in + n_mix_in]
    o_ref, mix_out = refs[n_ffn_in + n_mix_in:n_ffn_in + n_mix_in + 2]
    scratch = refs[n_ffn_in + n_mix_in + 2:]
    ffn_scratch, mix_scratch = scratch[:n_ffn_scratch], scratch[n_ffn_scratch:]
    side = []
    for s in range(mix_out.shape[0]):
        block = (tuple(r.at[s] for r in mix_in[:N_CTX_BLOCK_OPERANDS])
                 + tuple(mix_in[N_CTX_BLOCK_OPERANDS:]) + (mix_out.at[s],) + tuple(mix_scratch))
        side.append(_mixer_items(_mixer_parts(block, layer, False, n, 0)))
    _ffn_kernel(*ffn_in, o_ref, *ffn_scratch, pre_outproj=False, side=side)


def _ffn_with_context_mixer(x, mods, norm_g, wi, wo, layer, which, tokens_per_cond, first_row,
                            pool_in, glu, q, k, v, wts):
    t = x.shape[0]
    tm = TM_FFN
    b, n, _ = pool_in.shape
    per_step = tm // FFN_SUB
    assert n == TB and b == (t // tm) * per_step
    tok = pl.BlockSpec((tm, D), lambda i: (i, 0))
    ffn_specs = [
        tok,
        _mod_spec(layer, which, tm, tokens_per_cond, first_row),
        _const_spec((None, None, 1, D), (layer, which, 0, 0)),
        _const_spec((None, D, 2 * D_FF), (layer, 0, 0)),
        _const_spec((None, D_FF, D), (layer, 0, 0)),
    ]
    ffn_args = [x, mods, norm_g, wi, wo]

    def seqs(rows, width):
        return pl.BlockSpec((per_step, rows, width), lambda i: (i, 0, 0))

    mix_specs = [seqs(HALO, POOL_WIDTH), seqs(TB, POOL_WIDTH), seqs(HALO, POOL_WIDTH),
                 seqs(HALO, CONV_WIDTH), seqs(TB, CONV_WIDTH), seqs(HALO, CONV_WIDTH),
                 seqs(TB, ATTN_WIDTH), seqs(TB, KV_WIDTH), seqs(TB, KV_WIDTH)]
    mix_args = [pool_in, pool_in, pool_in, glu, glu, glu, q, k, v]
    assert len(mix_specs) == N_CTX_BLOCK_OPERANDS
    mix_specs += _mixer_weight_specs(layer)
    mix_args += _mixer_weight_args(wts)
    ffn_scratch = [pltpu.VMEM((tm // FFN_SUB, FFN_SUB, D_FF), BF16),
                   pltpu.VMEM((tm // FFN_SUB, FFN_SUB, D), BF16)]
    return pl.pallas_call(
        functools.partial(_ffn_side_kernel, n_ffn_in=len(ffn_specs), n_mix_in=len(mix_specs),
                          n_ffn_scratch=len(ffn_scratch), layer=layer, n=n),
        grid=(t // tm,),
        in_specs=ffn_specs + mix_specs,
        out_specs=[tok, pl.BlockSpec((per_step, TB, MIX_WIDTH), lambda i: (i, 0, 0))],
        out_shape=[jax.ShapeDtypeStruct((t, D), F32),
                   jax.ShapeDtypeStruct((b, n, MIX_WIDTH), BF16)],
        scratch_shapes=ffn_scratch + _mixer_scratch(n),
        compiler_params=pltpu.CompilerParams(
            dimension_semantics=("arbitrary",), vmem_limit_bytes=VMEM_LIMIT),
        name="ffn_with_context_mixer",
    )(*ffn_args, *mix_args)


def _rope_tables(n):
    t = np.arange(n)
    half = HEAD_DIM // 2
    inv = ROPE_THETA ** (-np.arange(0, half, 2, dtype=np.float64) / half)
    zeros = np.zeros((n, half // 2))
    cos, s_up, s_dn = [], [], []
    for pos in (t // GRID_W, t % GRID_W):
        ang = pos[:, None].astype(np.float64) * inv[None, :]
        cos += [np.cos(ang), np.cos(ang)]
        s_up += [-np.sin(ang), zeros]
        s_dn += [zeros, np.sin(ang)]
    return tuple(jnp.asarray(np.concatenate(parts * N_KV_HEADS, axis=1), dtype=F32)
                 for parts in (cos, s_up, s_dn))


def _block_diag_mean(width):
    idx = jnp.arange(width) // HEAD_DIM
    return jnp.where(idx[:, None] == idx[None, :], 1.0 / HEAD_DIM, 0.0).astype(BF16)


def kernel(x_prompt, x_sample, cache_k, cache_v, c, c_ctx, mod_w, mod_b, norm_g, ffn1_wi, ffn1_wo, ffn2_wi, ffn2_wo, w_in, w_out, pool_w, pool_scale, conv_dw, conv_b, conv_norm_g, conv_pw, q_norm_g, k_norm_g, sink):
    batch, seq, _ = x_prompt.shape
    dec_batch, dec_seq, _ = x_sample.shape
    past = cache_k.shape[2]
    assert 1 + dec_batch <= COND_ROWS

    cond = jnp.zeros((COND_ROWS, D), F32).at[0].set(c_ctx).at[1:1 + dec_batch].set(c)
    mods = _modulations(cond, mod_w, mod_b).reshape(DEPTH, COND_ROWS, 1, N_MOD * D)

    norm_g4 = norm_g.reshape(DEPTH, 3, 1, D)
    wi1, wo1 = ffn1_wi.astype(BF16), ffn1_wo.astype(BF16)
    wi2, wo2 = ffn2_wi.astype(BF16), ffn2_wo.astype(BF16)
    w_in_b, w_out_b = w_in.astype(BF16), w_out.astype(BF16)
    eye = jnp.eye(POOL_GROUPS, dtype=F32)
    pool_bd = (eye[None, :, None, :, None] * pool_w[:, :, :, None, :]).reshape(
        DEPTH, POOL_WIDTH, POOL_WIDTH).astype(BF16)
    wts = {
        "pool_bd": pool_bd,
        "pool_scale": pool_scale.reshape(DEPTH, 1, POOL_WIDTH),
        "conv_dw": conv_dw,
        "conv_b": conv_b.reshape(DEPTH, 1, CONV_WIDTH),
        "conv_norm_g": conv_norm_g.reshape(DEPTH, 1, CONV_WIDTH),
        "conv_pw": conv_pw.astype(BF16),
        "sink": sink.reshape(DEPTH * N_HEADS),
    }
    mq, mk = _block_diag_mean(ATTN_WIDTH), _block_diag_mean(KV_WIDTH)
    qg = jnp.tile(q_norm_g, (1, N_HEADS)).reshape(DEPTH, 1, ATTN_WIDTH)
    kg = jnp.tile(k_norm_g, (1, N_KV_HEADS)).reshape(DEPTH, 1, KV_WIDTH)
    rope = _rope_tables(dec_seq)
    cache_k4 = cache_k.reshape(dec_batch, DEPTH, past, KV_WIDTH)
    cache_v4 = cache_v.reshape(dec_batch, DEPTH, past, KV_WIDTH)

    def seqs(a, bsz, n):
        return a.reshape(bsz, n, a.shape[-1])

    yp = x_prompt.reshape(batch * seq, D)
    ys = x_sample.reshape(dec_batch * dec_seq, D)
    ks, vs = [], []
    for layer in range(DEPTH):
        yp = _ffn(yp, mods, norm_g4, wi1, wo1, layer, 0, None, 0)
        c_parts = _proj(yp, mods, norm_g4, w_in_b, mq, mk, qg, kg, layer, None, 0)
        ys, mix_c = _ffn_with_context_mixer(
            ys, mods, norm_g4, wi1, wo1, layer, 0, dec_seq, 1,
            *(seqs(a, batch, seq) for a in c_parts), wts)
        l_parts = _proj(ys, mods, norm_g4, w_in_b, mq, mk, qg, kg, layer, dec_seq, 1,
                        rope_tables=rope)
        yp = _ffn(yp, mods, norm_g4, wi2, wo2, layer, 2, None, 0,
                  mix=mix_c.reshape(batch * seq, MIX_WIDTH), w_out=w_out_b)
        mix_l = _mixer(*(seqs(a, dec_batch, dec_seq) for a in l_parts), wts, layer,
                       cache=(cache_k4, cache_v4))
        ys = _ffn(ys, mods, norm_g4, wi2, wo2, layer, 2, dec_seq, 1,
                  mix=mix_l.reshape(dec_batch * dec_seq, MIX_WIDTH), w_out=w_out_b)
        ks.append(c_parts[3].reshape(batch, seq, N_KV_HEADS, HEAD_DIM))
        vs.append(c_parts[4].reshape(batch, seq, N_KV_HEADS, HEAD_DIM))
    return (yp.reshape(batch, seq, D), ys.reshape(dec_batch, dec_seq, D),
            jnp.stack(ks, axis=1), jnp.stack(vs, axis=1))
```

```python
import functools

import jax
import jax.numpy as jnp
import numpy as np
from jax import lax
from jax.experimental import pallas as pl
from jax.experimental.pallas import tpu as pltpu

D = 1024
DEPTH = 2
GRID_W = 64
POOL_WIDTH = 256
POOL_GROUPS = 4
POOL_GROUP_DIM = POOL_WIDTH // POOL_GROUPS
CONV_WIDTH = 256
CONV_TAPS = 31
N_HEADS = 8
N_KV_HEADS = 2
HEAD_DIM = 64
Q_GROUP = N_HEADS // N_KV_HEADS
ATTN_WIDTH = N_HEADS * HEAD_DIM
KV_WIDTH = N_KV_HEADS * HEAD_DIM
MIX_WIDTH = POOL_WIDTH + CONV_WIDTH + ATTN_WIDTH
ATTN_OFFSET = POOL_WIDTH + 2 * CONV_WIDTH
IN_WIDTH = ATTN_OFFSET + ATTN_WIDTH + 2 * KV_WIDTH
BLOCK = 128
D_FF = 2816
N_MOD = 9
ROPE_THETA = 10000.0
EPS = 1e-6
NEG_INF = -1e30
SCALE = HEAD_DIM ** -0.5
LOG2E = 1.4426950408889634

COND_ROWS = 8
HALO = 16
TM_FFN = 1024
FFN_SUB = 512
FFN_PIECE = 128
TM_PROJ = 512
MXU_TILE = 256
FFN_CHUNK = 2 * MXU_TILE
TB = 256
ROW_CHUNK = 64
ZROWS = TB + 24
VMEM_LIMIT = 56 * 1024 * 1024

F32 = jnp.float32
BF16 = jnp.bfloat16


def _dot(a, b):
    return jnp.dot(a, b, preferred_element_type=F32)


def _rms_mod(x, g, sc, sh):
    ms = jnp.mean(x * x, axis=-1, keepdims=True)
    return (x * lax.rsqrt(ms + EPS) * g) * (1.0 + sc) + sh


def _mod_kernel(cond_ref, w_ref, b_ref, o_ref):
    c = cond_ref[...]
    s = (c * jax.nn.sigmoid(c)).astype(BF16)
    o_ref[...] = _dot(s, w_ref[...].astype(BF16)) + b_ref[...]


def _modulations(cond, mod_w, mod_b):
    tn = 1024
    nt = (N_MOD * D) // tn
    return pl.pallas_call(
        _mod_kernel,
        grid=(DEPTH, nt),
        in_specs=[
            pl.BlockSpec((COND_ROWS, D), lambda l, j: (0, 0)),
            pl.BlockSpec((None, D, tn), lambda l, j: (l, 0, j)),
            pl.BlockSpec((None, 1, tn), lambda l, j: (l, 0, j)),
        ],
        out_specs=pl.BlockSpec((None, COND_ROWS, tn), lambda l, j: (l, 0, j)),
        out_shape=jax.ShapeDtypeStruct((DEPTH, COND_ROWS, N_MOD * D), F32),
        compiler_params=pltpu.CompilerParams(
            dimension_semantics=("arbitrary", "arbitrary"),
            vmem_limit_bytes=VMEM_LIMIT),
        name="modulations",
    )(cond, mod_w, mod_b.reshape(DEPTH, 1, N_MOD * D))


def _cond_row(i, tm, tokens_per_cond, first_row):
    if tokens_per_cond is None:
        return first_row
    return first_row + i // (tokens_per_cond // tm)


def _mod_spec(layer, which, tm, tokens_per_cond, first_row):
    return pl.BlockSpec(
        (None, None, 1, 3 * D),
        lambda i: (layer, _cond_row(i, tm, tokens_per_cond, first_row), 0, which))


def _const_spec(shape, index):
    return pl.BlockSpec(shape, lambda i: index, pipeline_mode=pl.Buffered(1))


def _ffn_kernel(*refs, pre_outproj, side=None):
    if pre_outproj:
        (x_ref, mix_ref, wout_ref, modp_ref, mod_ref, g_ref, wi_ref, wo_ref,
         o_ref, a_ref, h_ref) = refs
        x1_ref = o_ref
    else:
        x_ref, mod_ref, g_ref, wi_ref, wo_ref, o_ref, a_ref, h_ref = refs
        x1_ref = x_ref
    n_sub = x_ref.shape[0] // FFN_SUB

    def rows(s):
        return slice(s * FFN_SUB, (s + 1) * FFN_SUB)

    def outproj(s):
        if pre_outproj:
            x1_ref[rows(s), :] = x_ref[rows(s), :] + modp_ref[:, 2 * D:3 * D] * _dot(
                mix_ref[rows(s), :], wout_ref[...])

    def prologue(s, piece):
        r0 = s * FFN_SUB + piece * FFN_PIECE
        p0 = piece * FFN_PIECE
        x = x1_ref[r0:r0 + FFN_PIECE, :]
        h = _rms_mod(x, g_ref[...], mod_ref[:, D:2 * D], mod_ref[:, 0:D]).astype(BF16)
        h_ref[s, p0:p0 + FFN_PIECE, :] = h
        token = h[0:16, :]
        for r in range(16, FFN_PIECE, 16):
            token = token + h[r:r + 16, :]
        return functools.reduce(lambda a, b: a + b, [token[:, l:l + 128] for l in range(0, D, 128)])

    def anchor(s, token):
        zero = jnp.zeros_like(token)
        h_ref[s, 0:16, 0:128] = h_ref[s, 0:16, 0:128] + jnp.maximum(jnp.minimum(token, zero), zero)

    n_piece = FFN_SUB // FFN_PIECE
    outproj(0)
    for piece in range(n_piece):
        prologue(0, piece)
    n_chunk = -(-D_FF // FFN_CHUNK)
    for s in range(n_sub):
        items = [] if side is None else side[s]
        per_chunk = -(-len(items) // n_chunk)
        if s + 1 < n_sub:
            outproj(s + 1)
        for c, c0 in enumerate(range(0, D_FF, FFN_CHUNK)):
            c1 = min(c0 + FFN_CHUNK, D_FF)
            gate = _dot(h_ref[s], wi_ref[:, c0:c1])
            up = _dot(h_ref[s], wi_ref[:, D_FF + c0:D_FF + c1])
            a_ref[s, :, c0:c1] = (gate * jax.nn.sigmoid(gate) * up).astype(BF16)
            if s + 1 < n_sub and c < n_piece:
                anchor(s, prologue(s + 1, c))
            for item in items[c * per_chunk:(c + 1) * per_chunk]:
                item()
        o_ref[rows(s), :] = x1_ref[rows(s), :] + (0.5 * mod_ref[:, 2 * D:3 * D]) * _dot(
            a_ref[s], wo_ref[...])


def _ffn(x, mods, norm_g, wi, wo, layer, which, tokens_per_cond, first_row,
         mix=None, w_out=None):
    t = x.shape[0]
    tm = TM_FFN
    tok = pl.BlockSpec((tm, D), lambda i: (i, 0))
    in_specs = [tok]
    args = [x]
    if mix is not None:
        in_specs += [tok, _const_spec((None, MIX_WIDTH, D), (layer, 0, 0)),
                     _mod_spec(layer, 1, tm, tokens_per_cond, first_row)]
        args += [mix, w_out, mods]
    in_specs += [
        _mod_spec(layer, which, tm, tokens_per_cond, first_row),
        _const_spec((None, None, 1, D), (layer, which, 0, 0)),
        _const_spec((None, D, 2 * D_FF), (layer, 0, 0)),
        _const_spec((None, D_FF, D), (layer, 0, 0)),
    ]
    args += [mods, norm_g, wi, wo]
    return pl.pallas_call(
        functools.partial(_ffn_kernel, pre_outproj=mix is not None),
        grid=(t // tm,),
        in_specs=in_specs,
        out_specs=tok,
        out_shape=jax.ShapeDtypeStruct((t, D), F32),
        scratch_shapes=[pltpu.VMEM((tm // FFN_SUB, FFN_SUB, D_FF), BF16),
                        pltpu.VMEM((tm // FFN_SUB, FFN_SUB, D), BF16)],
        compiler_params=pltpu.CompilerParams(
            dimension_semantics=("arbitrary",), vmem_limit_bytes=VMEM_LIMIT),
        name="ffn_outproj" if mix is not None else "ffn",
    )(*args)


def _head_norm(x, m, g):
    ms = _dot((x * x).astype(BF16), m)
    return x * lax.rsqrt(ms + EPS) * g


def _rope(x, c, s_up, s_dn):
    n = x.shape[1]
    return x * c + pltpu.roll(x, n - 16, 1) * s_up + pltpu.roll(x, 16, 1) * s_dn


def _proj_kernel(*refs, rope):
    if rope:
        (x_ref, mod_ref, g_ref, win_ref, mq_ref, mk_ref, qg_ref, kg_ref,
         c_ref, su_ref, sd_ref, pool_ref, glu_ref, q_ref, k_ref, v_ref) = refs
    else:
        (x_ref, mod_ref, g_ref, win_ref, mq_ref, mk_ref, qg_ref, kg_ref,
         pool_ref, glu_ref, q_ref, k_ref, v_ref) = refs
    h = _rms_mod(x_ref[...], g_ref[...], mod_ref[:, D:2 * D], mod_ref[:, 0:D]).astype(BF16)
    u = _dot(h, win_ref[...])
    pool_ref[...] = u[:, :POOL_WIDTH]
    glu_ref[...] = (u[:, POOL_WIDTH:POOL_WIDTH + CONV_WIDTH]
                    * jax.nn.sigmoid(u[:, POOL_WIDTH + CONV_WIDTH:ATTN_OFFSET]))
    q = _head_norm(u[:, ATTN_OFFSET:ATTN_OFFSET + ATTN_WIDTH], mq_ref[...], qg_ref[...])
    k = _head_norm(u[:, ATTN_OFFSET + ATTN_WIDTH:ATTN_OFFSET + ATTN_WIDTH + KV_WIDTH],
                   mk_ref[...], kg_ref[...])
    if rope:
        c, su, sd = c_ref[...], su_ref[...], sd_ref[...]
        k = _rope(k, c, su, sd)
        rep = ATTN_WIDTH // KV_WIDTH
        q = _rope(q, jnp.concatenate([c] * rep, axis=1), jnp.concatenate([su] * rep, axis=1),
                  jnp.concatenate([sd] * rep, axis=1))
    q_ref[...] = q.astype(BF16)
    k_ref[...] = k
    v_ref[...] = u[:, ATTN_OFFSET + ATTN_WIDTH + KV_WIDTH:]


def _proj(x, mods, norm_g, w_in, mq, mk, qg, kg, layer, tokens_per_cond, first_row,
          rope_tables=None):
    t = x.shape[0]
    tm = TM_PROJ
    in_specs = [
        pl.BlockSpec((tm, D), lambda i: (i, 0)),
        _mod_spec(layer, 1, tm, tokens_per_cond, first_row),
        _const_spec((None, None, 1, D), (layer, 1, 0, 0)),
        _const_spec((None, D, IN_WIDTH), (layer, 0, 0)),
        _const_spec((ATTN_WIDTH, ATTN_WIDTH), (0, 0)),
        _const_spec((KV_WIDTH, KV_WIDTH), (0, 0)),
        _const_spec((None, 1, ATTN_WIDTH), (layer, 0, 0)),
        _const_spec((None, 1, KV_WIDTH), (layer, 0, 0)),
    ]
    args = [x, mods, norm_g, w_in, mq, mk, qg, kg]
    if rope_tables is not None:
        seq_tiles = rope_tables[0].shape[0] // tm
        in_specs += [pl.BlockSpec((tm, KV_WIDTH), lambda i: (i % seq_tiles, 0))] * 3
        args += list(rope_tables)

    def out(width):
        return pl.BlockSpec((tm, width), lambda i: (i, 0))

    return pl.pallas_call(
        functools.partial(_proj_kernel, rope=rope_tables is not None),
        grid=(t // tm,),
        in_specs=in_specs,
        out_specs=[out(POOL_WIDTH), out(CONV_WIDTH), out(ATTN_WIDTH), out(KV_WIDTH), out(KV_WIDTH)],
        out_shape=[
            jax.ShapeDtypeStruct((t, POOL_WIDTH), F32),
            jax.ShapeDtypeStruct((t, CONV_WIDTH), F32),
            jax.ShapeDtypeStruct((t, ATTN_WIDTH), BF16),
            jax.ShapeDtypeStruct((t, KV_WIDTH), F32),
            jax.ShapeDtypeStruct((t, KV_WIDTH), F32),
        ],
        compiler_params=pltpu.CompilerParams(
            dimension_semantics=("arbitrary",), vmem_limit_bytes=VMEM_LIMIT),
        name="proj_rope" if rope_tables is not None else "proj",
    )(*args)


def _fill_ext(ext_ref, prev_ref, cur_ref, next_ref, i, nb):
    ext_ref[0:HALO, :] = jnp.where(i > 0, prev_ref[...], 0.0)
    ext_ref[HALO:HALO + TB, :] = cur_ref[...]
    ext_ref[HALO + TB:, :] = jnp.where(i < nb - 1, next_ref[...], 0.0)


def _shift_copies(ext_ref, z_ref, shifts, lanes):
    for b in shifts:
        z_ref[b, :, lanes] = ext_ref[b:b + ZROWS, lanes]


def _window(z_ref, ext_off, r0, lanes):
    a, b = divmod(ext_off, 8)
    return z_ref[b, 8 * a + r0:8 * a + r0 + ROW_CHUNK, lanes]


def _pool_chunk(z_ref, pooled_ref, r0, i, n):
    lo_lanes, hi_lanes = slice(0, 128), slice(128, 256)
    first = lax.broadcasted_iota(jnp.int32, (ROW_CHUNK, 128), 1) < POOL_GROUP_DIM
    t = i * TB + r0 + lax.broadcasted_iota(jnp.int32, (ROW_CHUNK, 128), 0)

    def centred(total, half, cur):
        count = jnp.minimum(t + half, n) - jnp.maximum(t - half, 0)
        return (total / count.astype(F32) - cur).astype(BF16)

    def u(off, lanes):
        return _window(z_ref, HALO + off, r0, lanes)

    cur = u(0, lo_lanes)
    acc2 = u(-1, lo_lanes) + cur
    acc4 = acc2 + u(-2, lo_lanes) + u(1, lo_lanes)
    pooled_ref[r0:r0 + ROW_CHUNK, lo_lanes] = centred(
        jnp.where(first, acc2, acc4), jnp.where(first, 1, 2), cur)
    cur = u(0, hi_lanes)
    acc8 = cur
    for off in (-4, -3, -2, -1, 1, 2, 3):
        acc8 = acc8 + u(off, hi_lanes)
    acc16 = acc8
    for off in (-8, -7, -6, -5, 4, 5, 6, 7):
        acc16 = acc16 + u(off, hi_lanes)
    pooled_ref[r0:r0 + ROW_CHUNK, hi_lanes] = centred(
        jnp.where(first, acc8, acc16), jnp.where(first, 4, 8), cur)


def _conv_chunk(z_ref, act_ref, r0, dw_ref, cb_ref, cg_ref):
    acc = None
    for k in range(CONV_TAPS):
        term = _window(z_ref, HALO - CONV_TAPS // 2 + k, r0, slice(None)) * dw_ref[k:k + 1, :]
        acc = term if acc is None else acc + term
    y = acc + cb_ref[...]
    ms = jnp.mean(y * y, axis=-1, keepdims=True)
    z = y * lax.rsqrt(ms + EPS) * cg_ref[...]
    act_ref[r0:r0 + ROW_CHUNK, :] = (z * jax.nn.sigmoid(z)).astype(BF16)


def _att_scores(s_ref, q_ref, q0, j, kb, biases):
    lane = lax.broadcasted_iota(jnp.int32, (BLOCK, KV_WIDTH), 1)
    own = (lane // HEAD_DIM) == j
    qs = []
    for h in range(j * Q_GROUP, (j + 1) * Q_GROUP):
        c0 = (h // 2) * KV_WIDTH
        qh = q_ref[q0:q0 + BLOCK, c0:c0 + KV_WIDTH]
        if h % 2 != j:
            qh = jnp.concatenate([qh[:, HEAD_DIM:], qh[:, :HEAD_DIM]], axis=1)
        qs.append(jnp.where(own, qh, jnp.zeros_like(qh)))
    qj = jnp.concatenate(qs, axis=0)
    s = lax.dot_general(kb, qj, (((1,), (1,)), ((), ())), preferred_element_type=F32)
    if biases is None:
        s_ref[...] = s
    else:
        s_ref[0:BLOCK] = s[0:BLOCK] + biases[0]
        s_ref[BLOCK:2 * BLOCK] = s[BLOCK:2 * BLOCK]
        s_ref[2 * BLOCK:3 * BLOCK] = s[2 * BLOCK:3 * BLOCK] + biases[1]
        s_ref[3 * BLOCK:] = s[3 * BLOCK:]


def _att_weights(p_ref, s_ref, j, sink_ref, layer):
    sink = jnp.concatenate(
        [jnp.full((1, BLOCK), sink_ref[layer * N_HEADS + j * Q_GROUP + g] * LOG2E, F32)
         for g in range(Q_GROUP)], axis=1)
    m = jnp.maximum(jnp.max(s_ref[...], axis=0, keepdims=True), sink)
    for r in range(0, s_ref.shape[0], BLOCK):
        p_ref[r:r + BLOCK] = jnp.exp2(s_ref[r:r + BLOCK] - m).astype(BF16)
    return jnp.exp2(sink - m)


def _att_values(p_ref, sink_w, j, vt):
    v_ones = jnp.concatenate(
        [vt[j * HEAD_DIM:(j + 1) * HEAD_DIM, :], jnp.ones((16, vt.shape[1]), BF16)], axis=0)
    o = _dot(v_ones, p_ref[...])
    return o[0:HEAD_DIM, :] / (o[HEAD_DIM:HEAD_DIM + 1, :] + sink_w)


N_ATT_STAGES = (TB // BLOCK) * N_KV_HEADS + 2


def _mixer_parts(refs, layer, latent, n, i):
    nb = n // TB
    (pp_ref, pc_ref, pn_ref, gp_ref, gc_ref, gn_ref, q_ref) = refs[:7]
    rest = refs[7:]
    if latent:
        (kp_ref, kc_ref, kn_ref, vp_ref, vc_ref, vn_ref, ck_ref, cv_ref) = rest[:8]
        rest = rest[8:]
    else:
        (ka_ref, va_ref) = rest[:2]
        rest = rest[2:]
    (poolw_ref, pscale_ref, dw_ref, cb_ref, cg_ref, pw_ref, sink_ref,
     o_ref, pext_ref, gext_ref, zp_ref, zg_ref, pooled_ref, act_ref,
     s_ref, p_ref, at_ref) = rest

    operands = []

    def prep():
        _fill_ext(pext_ref, pp_ref, pc_ref, pn_ref, i, nb)
        _shift_copies(pext_ref, zp_ref, (6, 7, 0, 1), slice(0, 128))
        _shift_copies(pext_ref, zp_ref, range(8), slice(128, 256))
        _fill_ext(gext_ref, gp_ref, gc_ref, gn_ref, i, nb)
        _shift_copies(gext_ref, zg_ref, range(8), slice(None))
        if latent:
            k_seq = jnp.concatenate([kp_ref[...], kc_ref[...], kn_ref[...]], axis=0).astype(BF16)
            vt_seq = jnp.concatenate(
                [vp_ref[...], vc_ref[...], vn_ref[...]], axis=0).T.astype(BF16)
            k_ctx, vt_ctx = ck_ref[...].astype(BF16), cv_ref[...].T.astype(BF16)
            cols = Q_GROUP * BLOCK
            key = lax.broadcasted_iota(jnp.int32, (BLOCK, cols), 0)
            r = lax.broadcasted_iota(jnp.int32, (BLOCK, cols), 1) % BLOCK
        else:
            k_all, vt_all = ka_ref[...].astype(BF16), va_ref[...].T.astype(BF16)
        for qb in range(TB // BLOCK):
            q0 = qb * BLOCK
            biases = None
            if latent:
                k_all = jnp.concatenate([k_seq[q0:q0 + 3 * BLOCK], k_ctx], axis=0)
                vt_all = jnp.concatenate([vt_seq[:, q0:q0 + 3 * BLOCK], vt_ctx], axis=1)
                blk = i * (TB // BLOCK) + qb
                biases = (jnp.where((key >= r) & (blk > 0), 0.0, NEG_INF).astype(F32),
                          jnp.where((key <= r) & (blk < n // BLOCK - 1), 0.0, NEG_INF).astype(F32))
            operands.append((q0, k_all, vt_all, biases))

    vpu_chunks = []
    for r0 in range(0, TB, ROW_CHUNK):
        vpu_chunks.append(functools.partial(_pool_chunk, zp_ref, pooled_ref, r0, i, n))
        vpu_chunks.append(functools.partial(_conv_chunk, zg_ref, act_ref, r0, dw_ref, cb_ref, cg_ref))

    def pool_conv_matmuls():
        o_ref[:, 0:POOL_WIDTH] = (
            _dot(pooled_ref[...], poolw_ref[...]) * pscale_ref[...]).astype(BF16)
        o_ref[:, POOL_WIDTH:POOL_WIDTH + CONV_WIDTH] = _dot(act_ref[...], pw_ref[...]).astype(BF16)

    units = [(qb, j) for qb in range(TB // BLOCK) for j in range(N_KV_HEADS)]
    sink_w = {}

    def att_stage(t):
        if t < len(units):
            qb, j = units[t]
            q0, k_all, _, biases = operands[qb]
            _att_scores(s_ref.at[t % 2], q_ref, q0, j, k_all, biases)
        if 0 <= t - 1 < len(units):
            u = t - 1
            sink_w[u] = _att_weights(p_ref.at[u % 2], s_ref.at[u % 2], units[u][1], sink_ref, layer)
        if 0 <= t - 2 < len(units):
            u = t - 2
            qb, j = units[u]
            o = _att_values(p_ref.at[u % 2], sink_w.pop(u), j, operands[qb][2])
            for g in range(Q_GROUP):
                c0 = (j * Q_GROUP + g) * HEAD_DIM
                at_ref[qb, c0:c0 + HEAD_DIM, :] = o[:, g * BLOCK:(g + 1) * BLOCK]

    def att_store():
        for qb in range(TB // BLOCK):
            o_ref[qb * BLOCK:(qb + 1) * BLOCK, POOL_WIDTH + CONV_WIDTH:] = (
                at_ref[qb].T.astype(BF16))

    return prep, att_stage, vpu_chunks, pool_conv_matmuls, att_store


def _mixer_items(parts):
    prep, att_stage, vpu_chunks, pool_conv_matmuls, att_store = parts
    per_stage = -(-len(vpu_chunks) // N_ATT_STAGES)

    def stage(t):
        att_stage(t)
        for chunk in vpu_chunks[t * per_stage:(t + 1) * per_stage]:
            chunk()

    def finish():
        pool_conv_matmuls()
        att_store()

    return [prep] + [functools.partial(stage, t) for t in range(N_ATT_STAGES)] + [finish]


def _mixer_kernel(*refs, layer, latent, n):
    prep, att_stage, vpu_chunks, pool_conv_matmuls, att_store = _mixer_parts(
        refs, layer, latent, n, pl.program_id(1))
    prep()
    if latent:
        for chunk in vpu_chunks:
            chunk()
        pool_conv_matmuls()
        for t in range(N_ATT_STAGES):
            att_stage(t)
    else:
        per_stage = -(-len(vpu_chunks) // N_ATT_STAGES)
        for t in range(N_ATT_STAGES):
            att_stage(t)
            for chunk in vpu_chunks[t * per_stage:(t + 1) * per_stage]:
                chunk()
        pool_conv_matmuls()
    att_store()


def _mixer(pool_in, glu, q, k, v, wts, layer, cache=None):
    b, n, _ = pool_in.shape
    nb = n // TB
    hpb = TB // HALO
    nh = n // HALO
    kpb = TB // BLOCK
    nk = n // BLOCK
    n_keys = n if cache is None else 3 * BLOCK + cache[0].shape[2]

    def cur(width):
        return pl.BlockSpec((None, TB, width), lambda bi, i: (bi, i, 0))

    def halo_prev(width):
        return pl.BlockSpec((None, HALO, width), lambda bi, i: (bi, jnp.maximum(i * hpb - 1, 0), 0))

    def halo_next(width):
        return pl.BlockSpec((None, HALO, width),
                            lambda bi, i: (bi, jnp.minimum((i + 1) * hpb, nh - 1), 0))

    in_specs = [halo_prev(POOL_WIDTH), cur(POOL_WIDTH), halo_next(POOL_WIDTH),
                halo_prev(CONV_WIDTH), cur(CONV_WIDTH), halo_next(CONV_WIDTH),
                cur(ATTN_WIDTH)]
    args = [pool_in, pool_in, pool_in, glu, glu, glu, q]
    if cache is not None:
        cache_k, cache_v = cache
        past = cache_k.shape[2]
        blk_prev = pl.BlockSpec((None, BLOCK, KV_WIDTH),
                                lambda bi, i: (bi, jnp.maximum(i * kpb - 1, 0), 0))
        blk_next = pl.BlockSpec((None, BLOCK, KV_WIDTH),
                                lambda bi, i: (bi, jnp.minimum((i + 1) * kpb, nk - 1), 0))
        cspec = pl.BlockSpec((None, None, past, KV_WIDTH), lambda bi, i: (bi, layer, 0, 0))
        in_specs += [blk_prev, cur(KV_WIDTH), blk_next, blk_prev, cur(KV_WIDTH), blk_next,
                     cspec, cspec]
        args += [k, k, k, v, v, v, cache_k, cache_v]
    else:
        whole = pl.BlockSpec((None, n, KV_WIDTH), lambda bi, i: (bi, 0, 0))
        in_specs += [whole, whole]
        args += [k, v]
    in_specs += _mixer_weight_specs(layer)
    args += _mixer_weight_args(wts)
    return pl.pallas_call(
        functools.partial(_mixer_kernel, layer=layer, latent=cache is not None, n=n),
        grid=(b, nb),
        in_specs=in_specs,
        out_specs=pl.BlockSpec((None, TB, MIX_WIDTH), lambda bi, i: (bi, i, 0)),
        out_shape=jax.ShapeDtypeStruct((b, n, MIX_WIDTH), BF16),
        scratch_shapes=_mixer_scratch(n_keys),
        compiler_params=pltpu.CompilerParams(
            dimension_semantics=("arbitrary", "arbitrary"), vmem_limit_bytes=VMEM_LIMIT),
        name="mixer_latent" if cache is not None else "mixer_context",
    )(*args)


def _mixer_weight_specs(layer):
    def const(shape, index):
        return pl.BlockSpec(shape, lambda *_: index)

    return [
        const((None, POOL_WIDTH, POOL_WIDTH), (layer, 0, 0)),
        const((None, 1, POOL_WIDTH), (layer, 0, 0)),
        const((None, CONV_TAPS, CONV_WIDTH), (layer, 0, 0)),
        const((None, 1, CONV_WIDTH), (layer, 0, 0)),
        const((None, 1, CONV_WIDTH), (layer, 0, 0)),
        const((None, CONV_WIDTH, CONV_WIDTH), (layer, 0, 0)),
        pl.BlockSpec(memory_space=pltpu.SMEM),
    ]


def _mixer_weight_args(wts):
    return [wts["pool_bd"], wts["pool_scale"], wts["conv_dw"], wts["conv_b"],
            wts["conv_norm_g"], wts["conv_pw"], wts["sink"]]


def _mixer_scratch(n_keys):
    return [pltpu.VMEM((TB + 2 * HALO, POOL_WIDTH), F32),
            pltpu.VMEM((TB + 2 * HALO, CONV_WIDTH), F32),
            pltpu.VMEM((8, ZROWS, POOL_WIDTH), F32),
            pltpu.VMEM((8, ZROWS, CONV_WIDTH), F32),
            pltpu.VMEM((TB, POOL_WIDTH), BF16),
            pltpu.VMEM((TB, CONV_WIDTH), BF16),
            pltpu.VMEM((2, n_keys, Q_GROUP * BLOCK), F32),
            pltpu.VMEM((2, n_keys, Q_GROUP * BLOCK), BF16),
            pltpu.VMEM((TB // BLOCK, ATTN_WIDTH, BLOCK), F32)]


N_CTX_BLOCK_OPERANDS = 9


def _ffn_side_kernel(*refs, n_ffn_in, n_mix_in, n_ffn_scratch, layer, n):
    ffn_in = refs[:n_ffn_in]
    mix_in = refs[n_ffn_in:n_ffn_in + n_mix_in]
    o_ref, mix_out = refs[n_ffn_in + n_mix_in:n_ffn_in + n_mix_in + 2]
    scratch = refs[n_ffn_in + n_mix_in + 2:]
    ffn_scratch, mix_scratch = scratch[:n_ffn_scratch], scratch[n_ffn_scratch:]
    side = []
    for s in range(mix_out.shape[0]):
        block = (tuple(r.at[s] for r in mix_in[:N_CTX_BLOCK_OPERANDS])
                 + tuple(mix_in[N_CTX_BLOCK_OPERANDS:]) + (mix_out.at[s],) + tuple(mix_scratch))
        side.append(_mixer_items(_mixer_parts(block, layer, False, n, 0)))
    _ffn_kernel(*ffn_in, o_ref, *ffn_scratch, pre_outproj=False, side=side)


def _ffn_with_context_mixer(x, mods, norm_g, wi, wo, layer, which, tokens_per_cond, first_row,
                            pool_in, glu, q, k, v, wts):
    t = x.shape[0]
    tm = TM_FFN
    b, n, _ = pool_in.shape
    per_step = tm // FFN_SUB
    assert n == TB and b == (t // tm) * per_step
    tok = pl.BlockSpec((tm, D), lambda i: (i, 0))
    ffn_specs = [
        tok,
        _mod_spec(layer, which, tm, tokens_per_cond, first_row),
        _const_spec((None, None, 1, D), (layer, which, 0, 0)),
        _const_spec((None, D, 2 * D_FF), (layer, 0, 0)),
        _const_spec((None, D_FF, D), (layer, 0, 0)),
    ]
    ffn_args = [x, mods, norm_g, wi, wo]

    def seqs(rows, width):
        return pl.BlockSpec((per_step, rows, width), lambda i: (i, 0, 0))

    mix_specs = [seqs(HALO, POOL_WIDTH), seqs(TB, POOL_WIDTH), seqs(HALO, POOL_WIDTH),
                 seqs(HALO, CONV_WIDTH), seqs(TB, CONV_WIDTH), seqs(HALO, CONV_WIDTH),
                 seqs(TB, ATTN_WIDTH), seqs(TB, KV_WIDTH), seqs(TB, KV_WIDTH)]
    mix_args = [pool_in, pool_in, pool_in, glu, glu, glu, q, k, v]
    assert len(mix_specs) == N_CTX_BLOCK_OPERANDS
    mix_specs += _mixer_weight_specs(layer)
    mix_args += _mixer_weight_args(wts)
    ffn_scratch = [pltpu.VMEM((tm // FFN_SUB, FFN_SUB, D_FF), BF16),
                   pltpu.VMEM((tm // FFN_SUB, FFN_SUB, D), BF16)]
    return pl.pallas_call(
        functools.partial(_ffn_side_kernel, n_ffn_in=len(ffn_specs), n_mix_in=len(mix_specs),
                          n_ffn_scratch=len(ffn_scratch), layer=layer, n=n),
        grid=(t // tm,),
        in_specs=ffn_specs + mix_specs,
        out_specs=[tok, pl.BlockSpec((per_step, TB, MIX_WIDTH), lambda i: (i, 0, 0))],
        out_shape=[jax.ShapeDtypeStruct((t, D), F32),
                   jax.ShapeDtypeStruct((b, n, MIX_WIDTH), BF16)],
        scratch_shapes=ffn_scratch + _mixer_scratch(n),
        compiler_params=pltpu.CompilerParams(
            dimension_semantics=("arbitrary",), vmem_limit_bytes=VMEM_LIMIT),
        name="ffn_with_context_mixer",
    )(*ffn_args, *mix_args)


def _rope_tables(n):
    t = np.arange(n)
    half = HEAD_DIM // 2
    inv = ROPE_THETA ** (-np.arange(0, half, 2, dtype=np.float64) / half)
    zeros = np.zeros((n, half // 2))
    cos, s_up, s_dn = [], [], []
    for pos in (t // GRID_W, t % GRID_W):
        ang = pos[:, None].astype(np.float64) * inv[None, :]
        cos += [np.cos(ang), np.cos(ang)]
        s_up += [-np.sin(ang), zeros]
        s_dn += [zeros, np.sin(ang)]
    return tuple(jnp.asarray(np.concatenate(parts * N_KV_HEADS, axis=1), dtype=F32)
                 for parts in (cos, s_up, s_dn))


def _block_diag_mean(width):
    idx = jnp.arange(width) // HEAD_DIM
    return jnp.where(idx[:, None] == idx[None, :], 1.0 / HEAD_DIM, 0.0).astype(BF16)


def kernel(x_prompt, x_sample, cache_k, cache_v, c, c_ctx, mod_w, mod_b, norm_g, ffn1_wi, ffn1_wo, ffn2_wi, ffn2_wo, w_in, w_out, pool_w, pool_scale, conv_dw, conv_b, conv_norm_g, conv_pw, q_norm_g, k_norm_g, sink):
    batch, seq, _ = x_prompt.shape
    dec_batch, dec_seq, _ = x_sample.shape
    past = cache_k.shape[2]
    assert 1 + dec_batch <= COND_ROWS

    cond = jnp.zeros((COND_ROWS, D), F32).at[0].set(c_ctx).at[1:1 + dec_batch].set(c)
    mods = _modulations(cond, mod_w, mod_b).reshape(DEPTH, COND_ROWS, 1, N_MOD * D)

    norm_g4 = norm_g.reshape(DEPTH, 3, 1, D)
    wi1, wo1 = ffn1_wi.astype(BF16), ffn1_wo.astype(BF16)
    wi2, wo2 = ffn2_wi.astype(BF16), ffn2_wo.astype(BF16)
    w_in_b, w_out_b = w_in.astype(BF16), w_out.astype(BF16)
    eye = jnp.eye(POOL_GROUPS, dtype=F32)
    pool_bd = (eye[None, :, None, :, None] * pool_w[:, :, :, None, :]).reshape(
        DEPTH, POOL_WIDTH, POOL_WIDTH).astype(BF16)
    wts = {
        "pool_bd": pool_bd,
        "pool_scale": pool_scale.reshape(DEPTH, 1, POOL_WIDTH),
        "conv_dw": conv_dw,
        "conv_b": conv_b.reshape(DEPTH, 1, CONV_WIDTH),
        "conv_norm_g": conv_norm_g.reshape(DEPTH, 1, CONV_WIDTH),
        "conv_pw": conv_pw.astype(BF16),
        "sink": sink.reshape(DEPTH * N_HEADS),
    }
    mq, mk = _block_diag_mean(ATTN_WIDTH), _block_diag_mean(KV_WIDTH)
    qg = jnp.tile(q_norm_g, (1, N_HEADS)).reshape(DEPTH, 1, ATTN_WIDTH) * (SCALE * LOG2E)
    kg = jnp.tile(k_norm_g, (1, N_KV_HEADS)).reshape(DEPTH, 1, KV_WIDTH)
    rope = _rope_tables(dec_seq)
    cache_k4 = cache_k.reshape(dec_batch, DEPTH, past, KV_WIDTH)
    cache_v4 = cache_v.reshape(dec_batch, DEPTH, past, KV_WIDTH)

    def seqs(a, bsz, n):
        return a.reshape(bsz, n, a.shape[-1])

    yp = x_prompt.reshape(batch * seq, D)
    ys = x_sample.reshape(dec_batch * dec_seq, D)
    ks, vs = [], []
    for layer in range(DEPTH):
        yp = _ffn(yp, mods, norm_g4, wi1, wo1, layer, 0, None, 0)
        c_parts = _proj(yp, mods, norm_g4, w_in_b, mq, mk, qg, kg, layer, None, 0)
        ys, mix_c = _ffn_with_context_mixer(
            ys, mods, norm_g4, wi1, wo1, layer, 0, dec_seq, 1,
            *(seqs(a, batch, seq) for a in c_parts), wts)
        l_parts = _proj(ys, mods, norm_g4, w_in_b, mq, mk, qg, kg, layer, dec_seq, 1,
                        rope_tables=rope)
        yp = _ffn(yp, mods, norm_g4, wi2, wo2, layer, 2, None, 0,
                  mix=mix_c.reshape(batch * seq, MIX_WIDTH), w_out=w_out_b)
        mix_l = _mixer(*(seqs(a, dec_batch, dec_seq) for a in l_parts), wts, layer,
                       cache=(cache_k4, cache_v4))
        ys = _ffn(ys, mods, norm_g4, wi2, wo2, layer, 2, dec_seq, 1,
                  mix=mix_l.reshape(dec_batch * dec_seq, MIX_WIDTH), w_out=w_out_b)
        ks.append(c_parts[3].reshape(batch, seq, N_KV_HEADS, HEAD_DIM))
        vs.append(c_parts[4].reshape(batch, seq, N_KV_HEADS, HEAD_DIM))
    return (yp.reshape(batch, seq, D), ys.reshape(dec_batch, dec_seq, D),
            jnp.stack(ks, axis=1), jnp.stack(vs, axis=1))
```

```python
import functools

import jax
import jax.numpy as jnp
import numpy as np
from jax import lax
from jax.experimental import pallas as pl
from jax.experimental.pallas import tpu as pltpu

D = 1024
DEPTH = 2
GRID_W = 64
POOL_WIDTH = 256
POOL_GROUPS = 4
POOL_GROUP_DIM = POOL_WIDTH // POOL_GROUPS
CONV_WIDTH = 256
CONV_TAPS = 31
N_HEADS = 8
N_KV_HEADS = 2
HEAD_DIM = 64
Q_GROUP = N_HEADS // N_KV_HEADS
ATTN_WIDTH = N_HEADS * HEAD_DIM
KV_WIDTH = N_KV_HEADS * HEAD_DIM
MIX_WIDTH = POOL_WIDTH + CONV_WIDTH + ATTN_WIDTH
ATTN_OFFSET = POOL_WIDTH + 2 * CONV_WIDTH
IN_WIDTH = ATTN_OFFSET + ATTN_WIDTH + 2 * KV_WIDTH
BLOCK = 128
D_FF = 2816
N_MOD = 9
ROPE_THETA = 10000.0
EPS = 1e-6
NEG_INF = -1e30
SCALE = HEAD_DIM ** -0.5
LOG2E = 1.4426950408889634

COND_ROWS = 8
HALO = 16
TM_FFN = 1024
FFN_SUB = 512
FFN_PIECE = 128
TM_PROJ = 512
MXU_TILE = 256
FFN_CHUNK = 2 * MXU_TILE
TB = 256
ROW_CHUNK = 64
ZROWS = TB + 24
VMEM_LIMIT = 56 * 1024 * 1024

F32 = jnp.float32
BF16 = jnp.bfloat16


def _dot(a, b):
    return jnp.dot(a, b, preferred_element_type=F32)


def _rms_mod(x, g, sc, sh):
    ms = jnp.mean(x * x, axis=-1, keepdims=True)
    return (x * lax.rsqrt(ms + EPS) * g) * (1.0 + sc) + sh


def _mod_kernel(cond_ref, w_ref, b_ref, o_ref):
    c = cond_ref[...]
    s = (c * jax.nn.sigmoid(c)).astype(BF16)
    o_ref[...] = _dot(s, w_ref[...].astype(BF16)) + b_ref[...]


def _modulations(cond, mod_w, mod_b):
    tn = 1024
    nt = (N_MOD * D) // tn
    return pl.pallas_call(
        _mod_kernel,
        grid=(DEPTH, nt),
        in_specs=[
            pl.BlockSpec((COND_ROWS, D), lambda l, j: (0, 0)),
            pl.BlockSpec((None, D, tn), lambda l, j: (l, 0, j)),
            pl.BlockSpec((None, 1, tn), lambda l, j: (l, 0, j)),
        ],
        out_specs=pl.BlockSpec((None, COND_ROWS, tn), lambda l, j: (l, 0, j)),
        out_shape=jax.ShapeDtypeStruct((DEPTH, COND_ROWS, N_MOD * D), F32),
        compiler_params=pltpu.CompilerParams(
            dimension_semantics=("arbitrary", "arbitrary"),
            vmem_limit_bytes=VMEM_LIMIT),
        name="modulations",
    )(cond, mod_w, mod_b.reshape(DEPTH, 1, N_MOD * D))


def _cond_row(i, tm, tokens_per_cond, first_row):
    if tokens_per_cond is None:
        return first_row
    return first_row + i // (tokens_per_cond // tm)


def _mod_spec(layer, which, tm, tokens_per_cond, first_row):
    return pl.BlockSpec(
        (None, None, 1, 3 * D),
        lambda i: (layer, _cond_row(i, tm, tokens_per_cond, first_row), 0, which))


def _const_spec(shape, index):
    return pl.BlockSpec(shape, lambda i: index, pipeline_mode=pl.Buffered(1))


def _ffn_kernel(*refs, pre_outproj, side=None):
    if pre_outproj:
        (x_ref, mix_ref, wout_ref, modp_ref, mod_ref, g_ref, wi_ref, wo_ref,
         o_ref, a_ref, h_ref) = refs
        x1_ref = o_ref
    else:
        x_ref, mod_ref, g_ref, wi_ref, wo_ref, o_ref, a_ref, h_ref = refs
        x1_ref = x_ref
    n_sub = x_ref.shape[0] // FFN_SUB

    def rows(s):
        return slice(s * FFN_SUB, (s + 1) * FFN_SUB)

    def outproj(s):
        if pre_outproj:
            x1_ref[rows(s), :] = x_ref[rows(s), :] + modp_ref[:, 2 * D:3 * D] * _dot(
                mix_ref[rows(s), :], wout_ref[...])

    def prologue(s, piece):
        r0 = s * FFN_SUB + piece * FFN_PIECE
        p0 = piece * FFN_PIECE
        x = x1_ref[r0:r0 + FFN_PIECE, :]
        h = _rms_mod(x, g_ref[...], mod_ref[:, D:2 * D], mod_ref[:, 0:D]).astype(BF16)
        h_ref[s, p0:p0 + FFN_PIECE, :] = h
        token = h[0:16, :]
        for r in range(16, FFN_PIECE, 16):
            token = token + h[r:r + 16, :]
        return functools.reduce(lambda a, b: a + b, [token[:, l:l + 128] for l in range(0, D, 128)])

    def anchor(s, token):
        zero = jnp.zeros_like(token)
        h_ref[s, 0:16, 0:128] = h_ref[s, 0:16, 0:128] + jnp.maximum(jnp.minimum(token, zero), zero)

    n_piece = FFN_SUB // FFN_PIECE
    outproj(0)
    for piece in range(n_piece):
        prologue(0, piece)
    n_chunk = -(-D_FF // FFN_CHUNK)
    for s in range(n_sub):
        items = [] if side is None else side[s]
        per_chunk = -(-len(items) // n_chunk)
        if s + 1 < n_sub:
            outproj(s + 1)
        for c, c0 in enumerate(range(0, D_FF, FFN_CHUNK)):
            c1 = min(c0 + FFN_CHUNK, D_FF)
            gate = _dot(h_ref[s], wi_ref[:, c0:c1])
            up = _dot(h_ref[s], wi_ref[:, D_FF + c0:D_FF + c1])
            a_ref[s, :, c0:c1] = (gate * jax.nn.sigmoid(gate) * up).astype(BF16)
            if s + 1 < n_sub and c < n_piece:
                anchor(s, prologue(s + 1, c))
            for item in items[c * per_chunk:(c + 1) * per_chunk]:
                item()
        o_ref[rows(s), :] = x1_ref[rows(s), :] + (0.5 * mod_ref[:, 2 * D:3 * D]) * _dot(
            a_ref[s], wo_ref[...])


def _with_casts(body, n_in, n_out, n_cast):
    if n_cast == 0:
        return body

    def kernel(*refs):
        ins, srcs = refs[:n_in], refs[n_in:n_in + n_cast]
        outs = refs[n_in + n_cast:n_in + n_cast + n_out]
        dsts = refs[n_in + n_cast + n_out:n_in + 2 * n_cast + n_out]
        for src, dst in zip(srcs, dsts):
            dst[...] = src[...].astype(BF16)
        body(*ins, *outs, *refs[n_in + 2 * n_cast + n_out:])

    return kernel


def _cast_specs(casts, steps):
    in_specs, out_specs, out_shapes, args = [], [], [], []
    for w, layer in casts:
        _, r, c = w.shape
        rows = r // steps
        assert r % steps == 0 and rows % 16 == 0
        in_specs.append(pl.BlockSpec((None, rows, c), lambda i, layer=layer: (layer, i, 0)))
        out_specs.append(pl.BlockSpec((rows, c), lambda i: (i, 0)))
        out_shapes.append(jax.ShapeDtypeStruct((r, c), BF16))
        args.append(w)
    return in_specs, out_specs, out_shapes, args


def _ffn(x, mods, norm_g, wi, wo, layer, which, tokens_per_cond, first_row,
         mix=None, w_out=None, casts=()):
    t = x.shape[0]
    tm = TM_FFN
    tok = pl.BlockSpec((tm, D), lambda i: (i, 0))
    in_specs = [tok]
    args = [x]
    if mix is not None:
        in_specs += [tok, _const_spec((MIX_WIDTH, D), (0, 0)),
                     _mod_spec(layer, 1, tm, tokens_per_cond, first_row)]
        args += [mix, w_out, mods]
    in_specs += [
        _mod_spec(layer, which, tm, tokens_per_cond, first_row),
        _const_spec((None, None, 1, D), (layer, which, 0, 0)),
        _const_spec((D, 2 * D_FF), (0, 0)),
        _const_spec((D_FF, D), (0, 0)),
    ]
    args += [mods, norm_g, wi, wo]
    c_in, c_out, c_shapes, c_args = _cast_specs(casts, t // tm)
    return pl.pallas_call(
        _with_casts(functools.partial(_ffn_kernel, pre_outproj=mix is not None),
                    len(in_specs), 1, len(casts)),
        grid=(t // tm,),
        in_specs=in_specs + c_in,
        out_specs=[tok] + c_out,
        out_shape=[jax.ShapeDtypeStruct((t, D), F32)] + c_shapes,
        scratch_shapes=[pltpu.VMEM((tm // FFN_SUB, FFN_SUB, D_FF), BF16),
                        pltpu.VMEM((tm // FFN_SUB, FFN_SUB, D), BF16)],
        compiler_params=pltpu.CompilerParams(
            dimension_semantics=("arbitrary",), vmem_limit_bytes=VMEM_LIMIT),
        name="ffn_outproj" if mix is not None else "ffn",
    )(*args, *c_args)


def _head_norm(x, m, g):
    ms = _dot((x * x).astype(BF16), m)
    return x * lax.rsqrt(ms + EPS) * g


def _rope(x, c, s_up, s_dn):
    n = x.shape[1]
    return x * c + pltpu.roll(x, n - 16, 1) * s_up + pltpu.roll(x, 16, 1) * s_dn


def _proj_kernel(*refs, rope):
    if rope:
        (x_ref, mod_ref, g_ref, win_ref, mq_ref, mk_ref, qg_ref, kg_ref,
         c_ref, su_ref, sd_ref, pool_ref, glu_ref, q_ref, k_ref, v_ref) = refs
    else:
        (x_ref, mod_ref, g_ref, win_ref, mq_ref, mk_ref, qg_ref, kg_ref,
         pool_ref, glu_ref, q_ref, k_ref, v_ref) = refs
    h = _rms_mod(x_ref[...], g_ref[...], mod_ref[:, D:2 * D], mod_ref[:, 0:D]).astype(BF16)
    u = _dot(h, win_ref[...])
    pool_ref[...] = u[:, :POOL_WIDTH]
    glu_ref[...] = (u[:, POOL_WIDTH:POOL_WIDTH + CONV_WIDTH]
                    * jax.nn.sigmoid(u[:, POOL_WIDTH + CONV_WIDTH:ATTN_OFFSET]))
    q = _head_norm(u[:, ATTN_OFFSET:ATTN_OFFSET + ATTN_WIDTH], mq_ref[...], qg_ref[...])
    k = _head_norm(u[:, ATTN_OFFSET + ATTN_WIDTH:ATTN_OFFSET + ATTN_WIDTH + KV_WIDTH],
                   mk_ref[...], kg_ref[...])
    if rope:
        c, su, sd = c_ref[...], su_ref[...], sd_ref[...]
        k = _rope(k, c, su, sd)
        rep = ATTN_WIDTH // KV_WIDTH
        q = _rope(q, jnp.concatenate([c] * rep, axis=1), jnp.concatenate([su] * rep, axis=1),
                  jnp.concatenate([sd] * rep, axis=1))
    q_ref[...] = q.astype(BF16)
    k_ref[...] = k
    v_ref[...] = u[:, ATTN_OFFSET + ATTN_WIDTH + KV_WIDTH:]


def _proj(x, mods, norm_g, w_in, mq, mk, qg, kg, layer, tokens_per_cond, first_row,
          rope_tables=None):
    t = x.shape[0]
    tm = TM_PROJ
    in_specs = [
        pl.BlockSpec((tm, D), lambda i: (i, 0)),
        _mod_spec(layer, 1, tm, tokens_per_cond, first_row),
        _const_spec((None, None, 1, D), (layer, 1, 0, 0)),
        _const_spec((D, IN_WIDTH), (0, 0)),
        _const_spec((ATTN_WIDTH, ATTN_WIDTH), (0, 0)),
        _const_spec((KV_WIDTH, KV_WIDTH), (0, 0)),
        _const_spec((None, 1, ATTN_WIDTH), (layer, 0, 0)),
        _const_spec((None, 1, KV_WIDTH), (layer, 0, 0)),
    ]
    args = [x, mods, norm_g, w_in, mq, mk, qg, kg]
    if rope_tables is not None:
        seq_tiles = rope_tables[0].shape[0] // tm
        in_specs += [pl.BlockSpec((tm, KV_WIDTH), lambda i: (i % seq_tiles, 0))] * 3
        args += list(rope_tables)

    def out(width):
        return pl.BlockSpec((tm, width), lambda i: (i, 0))

    return pl.pallas_call(
        functools.partial(_proj_kernel, rope=rope_tables is not None),
        grid=(t // tm,),
        in_specs=in_specs,
        out_specs=[out(POOL_WIDTH), out(CONV_WIDTH), out(ATTN_WIDTH), out(KV_WIDTH), out(KV_WIDTH)],
        out_shape=[
            jax.ShapeDtypeStruct((t, POOL_WIDTH), F32),
            jax.ShapeDtypeStruct((t, CONV_WIDTH), F32),
            jax.ShapeDtypeStruct((t, ATTN_WIDTH), BF16),
            jax.ShapeDtypeStruct((t, KV_WIDTH), F32),
            jax.ShapeDtypeStruct((t, KV_WIDTH), F32),
        ],
        compiler_params=pltpu.CompilerParams(
            dimension_semantics=("arbitrary",), vmem_limit_bytes=VMEM_LIMIT),
        name="proj_rope" if rope_tables is not None else "proj",
    )(*args)


def _fill_ext(ext_ref, prev_ref, cur_ref, next_ref, i, nb):
    ext_ref[0:HALO, :] = jnp.where(i > 0, prev_ref[...], 0.0)
    ext_ref[HALO:HALO + TB, :] = cur_ref[...]
    ext_ref[HALO + TB:, :] = jnp.where(i < nb - 1, next_ref[...], 0.0)


def _shift_copies(ext_ref, z_ref, shifts, lanes):
    for b in shifts:
        z_ref[b, :, lanes] = ext_ref[b:b + ZROWS, lanes]


def _window(z_ref, ext_off, r0, lanes):
    a, b = divmod(ext_off, 8)
    return z_ref[b, 8 * a + r0:8 * a + r0 + ROW_CHUNK, lanes]


def _pool_chunk(z_ref, pooled_ref, r0, i, n):
    lo_lanes, hi_lanes = slice(0, 128), slice(128, 256)
    first = lax.broadcasted_iota(jnp.int32, (ROW_CHUNK, 128), 1) < POOL_GROUP_DIM
    t = i * TB + r0 + lax.broadcasted_iota(jnp.int32, (ROW_CHUNK, 128), 0)

    def centred(total, half, cur):
        count = jnp.minimum(t + half, n) - jnp.maximum(t - half, 0)
        return (total / count.astype(F32) - cur).astype(BF16)

    def u(off, lanes):
        return _window(z_ref, HALO + off, r0, lanes)

    cur = u(0, lo_lanes)
    acc2 = u(-1, lo_lanes) + cur
    acc4 = acc2 + u(-2, lo_lanes) + u(1, lo_lanes)
    pooled_ref[r0:r0 + ROW_CHUNK, lo_lanes] = centred(
        jnp.where(first, acc2, acc4), jnp.where(first, 1, 2), cur)
    cur = u(0, hi_lanes)
    acc8 = cur
    for off in (-4, -3, -2, -1, 1, 2, 3):
        acc8 = acc8 + u(off, hi_lanes)
    acc16 = acc8
    for off in (-8, -7, -6, -5, 4, 5, 6, 7):
        acc16 = acc16 + u(off, hi_lanes)
    pooled_ref[r0:r0 + ROW_CHUNK, hi_lanes] = centred(
        jnp.where(first, acc8, acc16), jnp.where(first, 4, 8), cur)


def _conv_chunk(z_ref, act_ref, r0, dw_ref, cb_ref, cg_ref):
    acc = None
    for k in range(CONV_TAPS):
        term = _window(z_ref, HALO - CONV_TAPS // 2 + k, r0, slice(None)) * dw_ref[k:k + 1, :]
        acc = term if acc is None else acc + term
    y = acc + cb_ref[...]
    ms = jnp.mean(y * y, axis=-1, keepdims=True)
    z = y * lax.rsqrt(ms + EPS) * cg_ref[...]
    act_ref[r0:r0 + ROW_CHUNK, :] = (z * jax.nn.sigmoid(z)).astype(BF16)


def _att_scores(s_ref, q_ref, q0, j, kb, biases):
    lane = lax.broadcasted_iota(jnp.int32, (BLOCK, KV_WIDTH), 1)
    own = (lane // HEAD_DIM) == j
    qs = []
    for h in range(j * Q_GROUP, (j + 1) * Q_GROUP):
        c0 = (h // 2) * KV_WIDTH
        qh = q_ref[q0:q0 + BLOCK, c0:c0 + KV_WIDTH]
        if h % 2 != j:
            qh = jnp.concatenate([qh[:, HEAD_DIM:], qh[:, :HEAD_DIM]], axis=1)
        qs.append(jnp.where(own, qh, jnp.zeros_like(qh)))
    qj = jnp.concatenate(qs, axis=0)
    s = lax.dot_general(kb, qj, (((1,), (1,)), ((), ())), preferred_element_type=F32)
    if biases is None:
        s_ref[...] = s
    else:
        s_ref[0:BLOCK] = s[0:BLOCK] + biases[0]
        s_ref[BLOCK:2 * BLOCK] = s[BLOCK:2 * BLOCK]
        s_ref[2 * BLOCK:3 * BLOCK] = s[2 * BLOCK:3 * BLOCK] + biases[1]
        s_ref[3 * BLOCK:] = s[3 * BLOCK:]


def _att_weights(p_ref, s_ref, j, sink_ref, layer):
    sink = jnp.concatenate(
        [jnp.full((1, BLOCK), sink_ref[layer * N_HEADS + j * Q_GROUP + g] * LOG2E, F32)
         for g in range(Q_GROUP)], axis=1)
    m = jnp.maximum(jnp.max(s_ref[...], axis=0, keepdims=True), sink)
    for r in range(0, s_ref.shape[0], BLOCK):
        p_ref[r:r + BLOCK] = jnp.exp2(s_ref[r:r + BLOCK] - m).astype(BF16)
    return jnp.exp2(sink - m)


def _att_values(p_ref, sink_w, j, vt):
    v_ones = jnp.concatenate(
        [vt[j * HEAD_DIM:(j + 1) * HEAD_DIM, :], jnp.ones((16, vt.shape[1]), BF16)], axis=0)
    o = _dot(v_ones, p_ref[...])
    return o[0:HEAD_DIM, :] / (o[HEAD_DIM:HEAD_DIM + 1, :] + sink_w)


N_ATT_STAGES = (TB // BLOCK) * N_KV_HEADS + 2


def _mixer_parts(refs, layer, latent, n, i):
    nb = n // TB
    (pp_ref, pc_ref, pn_ref, gp_ref, gc_ref, gn_ref, q_ref) = refs[:7]
    rest = refs[7:]
    if latent:
        (kp_ref, kc_ref, kn_ref, vp_ref, vc_ref, vn_ref, ck_ref, cv_ref) = rest[:8]
        rest = rest[8:]
    else:
        (ka_ref, va_ref) = rest[:2]
        rest = rest[2:]
    (poolw_ref, pscale_ref, dw_ref, cb_ref, cg_ref, pw_ref, sink_ref,
     o_ref, pext_ref, gext_ref, zp_ref, zg_ref, pooled_ref, act_ref,
     s_ref, p_ref, at_ref) = rest

    operands = []

    def prep():
        _fill_ext(pext_ref, pp_ref, pc_ref, pn_ref, i, nb)
        _shift_copies(pext_ref, zp_ref, (6, 7, 0, 1), slice(0, 128))
        _shift_copies(pext_ref, zp_ref, range(8), slice(128, 256))
        _fill_ext(gext_ref, gp_ref, gc_ref, gn_ref, i, nb)
        _shift_copies(gext_ref, zg_ref, range(8), slice(None))
        if latent:
            k_seq = jnp.concatenate([kp_ref[...], kc_ref[...], kn_ref[...]], axis=0).astype(BF16)
            vt_seq = jnp.concatenate(
                [vp_ref[...], vc_ref[...], vn_ref[...]], axis=0).T.astype(BF16)
            k_ctx, vt_ctx = ck_ref[...].astype(BF16), cv_ref[...].T.astype(BF16)
            cols = Q_GROUP * BLOCK
            key = lax.broadcasted_iota(jnp.int32, (BLOCK, cols), 0)
            r = lax.broadcasted_iota(jnp.int32, (BLOCK, cols), 1) % BLOCK
        else:
            k_all, vt_all = ka_ref[...].astype(BF16), va_ref[...].T.astype(BF16)
        for qb in range(TB // BLOCK):
            q0 = qb * BLOCK
            biases = None
            if latent:
                k_all = jnp.concatenate([k_seq[q0:q0 + 3 * BLOCK], k_ctx], axis=0)
                vt_all = jnp.concatenate([vt_seq[:, q0:q0 + 3 * BLOCK], vt_ctx], axis=1)
                blk = i * (TB // BLOCK) + qb
                biases = (jnp.where((key >= r) & (blk > 0), 0.0, NEG_INF).astype(F32),
                          jnp.where((key <= r) & (blk < n // BLOCK - 1), 0.0, NEG_INF).astype(F32))
            operands.append((q0, k_all, vt_all, biases))

    vpu_chunks = []
    for r0 in range(0, TB, ROW_CHUNK):
        vpu_chunks.append(functools.partial(_pool_chunk, zp_ref, pooled_ref, r0, i, n))
        vpu_chunks.append(functools.partial(_conv_chunk, zg_ref, act_ref, r0, dw_ref, cb_ref, cg_ref))

    def pool_conv_matmuls():
        o_ref[:, 0:POOL_WIDTH] = (
            _dot(pooled_ref[...], poolw_ref[...]) * pscale_ref[...]).astype(BF16)
        o_ref[:, POOL_WIDTH:POOL_WIDTH + CONV_WIDTH] = _dot(act_ref[...], pw_ref[...]).astype(BF16)

    units = [(qb, j) for qb in range(TB // BLOCK) for j in range(N_KV_HEADS)]
    sink_w = {}

    def att_stage(t):
        if t < len(units):
            qb, j = units[t]
            q0, k_all, _, biases = operands[qb]
            _att_scores(s_ref.at[t % 2], q_ref, q0, j, k_all, biases)
        if 0 <= t - 1 < len(units):
            u = t - 1
            sink_w[u] = _att_weights(p_ref.at[u % 2], s_ref.at[u % 2], units[u][1], sink_ref, layer)
        if 0 <= t - 2 < len(units):
            u = t - 2
            qb, j = units[u]
            o = _att_values(p_ref.at[u % 2], sink_w.pop(u), j, operands[qb][2])
            for g in range(Q_GROUP):
                c0 = (j * Q_GROUP + g) * HEAD_DIM
                at_ref[qb, c0:c0 + HEAD_DIM, :] = o[:, g * BLOCK:(g + 1) * BLOCK]

    def att_store():
        for qb in range(TB // BLOCK):
            o_ref[qb * BLOCK:(qb + 1) * BLOCK, POOL_WIDTH + CONV_WIDTH:] = (
                at_ref[qb].T.astype(BF16))

    return prep, att_stage, vpu_chunks, pool_conv_matmuls, att_store


def _mixer_items(parts):
    prep, att_stage, vpu_chunks, pool_conv_matmuls, att_store = parts
    per_stage = -(-len(vpu_chunks) // N_ATT_STAGES)

    def stage(t):
        att_stage(t)
        for chunk in vpu_chunks[t * per_stage:(t + 1) * per_stage]:
            chunk()

    def finish():
        pool_conv_matmuls()
        att_store()

    return [prep] + [functools.partial(stage, t) for t in range(N_ATT_STAGES)] + [finish]


def _mixer_kernel(*refs, layer, latent, n):
    prep, att_stage, vpu_chunks, pool_conv_matmuls, att_store = _mixer_parts(
        refs, layer, latent, n, pl.program_id(1))
    prep()
    if latent:
        for chunk in vpu_chunks:
            chunk()
        pool_conv_matmuls()
        for t in range(N_ATT_STAGES):
            att_stage(t)
    else:
        per_stage = -(-len(vpu_chunks) // N_ATT_STAGES)
        for t in range(N_ATT_STAGES):
            att_stage(t)
            for chunk in vpu_chunks[t * per_stage:(t + 1) * per_stage]:
                chunk()
        pool_conv_matmuls()
    att_store()


def _mixer(pool_in, glu, q, k, v, wts, layer, cache=None):
    b, n, _ = pool_in.shape
    nb = n // TB
    hpb = TB // HALO
    nh = n // HALO
    kpb = TB // BLOCK
    nk = n // BLOCK
    n_keys = n if cache is None else 3 * BLOCK + cache[0].shape[2]

    def cur(width):
        return pl.BlockSpec((None, TB, width), lambda bi, i: (bi, i, 0))

    def halo_prev(width):
        return pl.BlockSpec((None, HALO, width), lambda bi, i: (bi, jnp.maximum(i * hpb - 1, 0), 0))

    def halo_next(width):
        return pl.BlockSpec((None, HALO, width),
                            lambda bi, i: (bi, jnp.minimum((i + 1) * hpb, nh - 1), 0))

    in_specs = [halo_prev(POOL_WIDTH), cur(POOL_WIDTH), halo_next(POOL_WIDTH),
                halo_prev(CONV_WIDTH), cur(CONV_WIDTH), halo_next(CONV_WIDTH),
                cur(ATTN_WIDTH)]
    args = [pool_in, pool_in, pool_in, glu, glu, glu, q]
    if cache is not None:
        cache_k, cache_v = cache
        past = cache_k.shape[2]
        blk_prev = pl.BlockSpec((None, BLOCK, KV_WIDTH),
                                lambda bi, i: (bi, jnp.maximum(i * kpb - 1, 0), 0))
        blk_next = pl.BlockSpec((None, BLOCK, KV_WIDTH),
                                lambda bi, i: (bi, jnp.minimum((i + 1) * kpb, nk - 1), 0))
        cspec = pl.BlockSpec((None, None, past, KV_WIDTH), lambda bi, i: (bi, layer, 0, 0))
        in_specs += [blk_prev, cur(KV_WIDTH), blk_next, blk_prev, cur(KV_WIDTH), blk_next,
                     cspec, cspec]
        args += [k, k, k, v, v, v, cache_k, cache_v]
    else:
        whole = pl.BlockSpec((None, n, KV_WIDTH), lambda bi, i: (bi, 0, 0))
        in_specs += [whole, whole]
        args += [k, v]
    in_specs += _mixer_weight_specs(layer)
    args += _mixer_weight_args(wts)
    return pl.pallas_call(
        functools.partial(_mixer_kernel, layer=layer, latent=cache is not None, n=n),
        grid=(b, nb),
        in_specs=in_specs,
        out_specs=pl.BlockSpec((None, TB, MIX_WIDTH), lambda bi, i: (bi, i, 0)),
        out_shape=jax.ShapeDtypeStruct((b, n, MIX_WIDTH), BF16),
        scratch_shapes=_mixer_scratch(n_keys),
        compiler_params=pltpu.CompilerParams(
            dimension_semantics=("arbitrary", "arbitrary"), vmem_limit_bytes=VMEM_LIMIT),
        name="mixer_latent" if cache is not None else "mixer_context",
    )(*args)


def _mixer_weight_specs(layer):
    def const(shape, index):
        return pl.BlockSpec(shape, lambda *_: index)

    return [
        const((None, POOL_WIDTH, POOL_WIDTH), (layer, 0, 0)),
        const((None, 1, POOL_WIDTH), (layer, 0, 0)),
        const((None, CONV_TAPS, CONV_WIDTH), (layer, 0, 0)),
        const((None, 1, CONV_WIDTH), (layer, 0, 0)),
        const((None, 1, CONV_WIDTH), (layer, 0, 0)),
        const((None, CONV_WIDTH, CONV_WIDTH), (layer, 0, 0)),
        pl.BlockSpec(memory_space=pltpu.SMEM),
    ]


def _mixer_weight_args(wts):
    return [wts["pool_bd"], wts["pool_scale"], wts["conv_dw"], wts["conv_b"],
            wts["conv_norm_g"], wts["conv_pw"], wts["sink"]]


def _mixer_scratch(n_keys):
    return [pltpu.VMEM((TB + 2 * HALO, POOL_WIDTH), F32),
            pltpu.VMEM((TB + 2 * HALO, CONV_WIDTH), F32),
            pltpu.VMEM((8, ZROWS, POOL_WIDTH), F32),
            pltpu.VMEM((8, ZROWS, CONV_WIDTH), F32),
            pltpu.VMEM((TB, POOL_WIDTH), BF16),
            pltpu.VMEM((TB, CONV_WIDTH), BF16),
            pltpu.VMEM((2, n_keys, Q_GROUP * BLOCK), F32),
            pltpu.VMEM((2, n_keys, Q_GROUP * BLOCK), BF16),
            pltpu.VMEM((TB // BLOCK, ATTN_WIDTH, BLOCK), F32)]


N_CTX_BLOCK_OPERANDS = 9


def _ffn_side_kernel(*refs, n_ffn_in, n_mix_in, n_ffn_scratch, layer, n):
    ffn_in = refs[:n_ffn_in]
    mix_in = refs[n_ffn_in:n_ffn_in + n_mix_in]
    o_ref, mix_out = refs[n_ffn_in + n_mix_in:n_ffn_in + n_mix_in + 2]
    scratch = refs[n_ffn_in + n_mix_in + 2:]
    ffn_scratch, mix_scratch = scratch[:n_ffn_scratch], scratch[n_ffn_scratch:]
    side = []
    for s in range(mix_out.shape[0]):
        block = (tuple(r.at[s] for r in mix_in[:N_CTX_BLOCK_OPERANDS])
                 + tuple(mix_in[N_CTX_BLOCK_OPERANDS:]) + (mix_out.at[s],) + tuple(mix_scratch))
        side.append(_mixer_items(_mixer_parts(block, layer, False, n, 0)))
    _ffn_kernel(*ffn_in, o_ref, *ffn_scratch, pre_outproj=False, side=side)


def _ffn_with_context_mixer(x, mods, norm_g, wi, wo, layer, which, tokens_per_cond, first_row,
                            pool_in, glu, q, k, v, wts, casts=()):
    t = x.shape[0]
    tm = TM_FFN
    b, n, _ = pool_in.shape
    per_step = tm // FFN_SUB
    assert n == TB and b == (t // tm) * per_step
    tok = pl.BlockSpec((tm, D), lambda i: (i, 0))
    ffn_specs = [
        tok,
        _mod_spec(layer, which, tm, tokens_per_cond, first_row),
        _const_spec((None, None, 1, D), (layer, which, 0, 0)),
        _const_spec((D, 2 * D_FF), (0, 0)),
        _const_spec((D_FF, D), (0, 0)),
    ]
    ffn_args = [x, mods, norm_g, wi, wo]

    def seqs(rows, width):
        return pl.BlockSpec((per_step, rows, width), lambda i: (i, 0, 0))

    mix_specs = [seqs(HALO, POOL_WIDTH), seqs(TB, POOL_WIDTH), seqs(HALO, POOL_WIDTH),
                 seqs(HALO, CONV_WIDTH), seqs(TB, CONV_WIDTH), seqs(HALO, CONV_WIDTH),
                 seqs(TB, ATTN_WIDTH), seqs(TB, KV_WIDTH), seqs(TB, KV_WIDTH)]
    mix_args = [pool_in, pool_in, pool_in, glu, glu, glu, q, k, v]
    assert len(mix_specs) == N_CTX_BLOCK_OPERANDS
    mix_specs += _mixer_weight_specs(layer)
    mix_args += _mixer_weight_args(wts)
    ffn_scratch = [pltpu.VMEM((tm // FFN_SUB, FFN_SUB, D_FF), BF16),
                   pltpu.VMEM((tm // FFN_SUB, FFN_SUB, D), BF16)]
    c_in, c_out, c_shapes, c_args = _cast_specs(casts, t // tm)
    body = functools.partial(_ffn_side_kernel, n_ffn_in=len(ffn_specs), n_mix_in=len(mix_specs),
                             n_ffn_scratch=len(ffn_scratch), layer=layer, n=n)
    return pl.pallas_call(
        _with_casts(body, len(ffn_specs) + len(mix_specs), 2, len(casts)),
        grid=(t // tm,),
        in_specs=ffn_specs + mix_specs + c_in,
        out_specs=[tok, pl.BlockSpec((per_step, TB, MIX_WIDTH), lambda i: (i, 0, 0))] + c_out,
        out_shape=[jax.ShapeDtypeStruct((t, D), F32),
                   jax.ShapeDtypeStruct((b, n, MIX_WIDTH), BF16)] + c_shapes,
        scratch_shapes=ffn_scratch + _mixer_scratch(n),
        compiler_params=pltpu.CompilerParams(
            dimension_semantics=("arbitrary",), vmem_limit_bytes=VMEM_LIMIT),
        name="ffn_with_context_mixer",
    )(*ffn_args, *mix_args, *c_args)


def _rope_tables(n):
    t = np.arange(n)
    half = HEAD_DIM // 2
    inv = ROPE_THETA ** (-np.arange(0, half, 2, dtype=np.float64) / half)
    zeros = np.zeros((n, half // 2))
    cos, s_up, s_dn = [], [], []
    for pos in (t // GRID_W, t % GRID_W):
        ang = pos[:, None].astype(np.float64) * inv[None, :]
        cos += [np.cos(ang), np.cos(ang)]
        s_up += [-np.sin(ang), zeros]
        s_dn += [zeros, np.sin(ang)]
    return tuple(jnp.asarray(np.concatenate(parts * N_KV_HEADS, axis=1), dtype=F32)
                 for parts in (cos, s_up, s_dn))


def _block_diag_mean(width):
    idx = jnp.arange(width) // HEAD_DIM
    return jnp.where(idx[:, None] == idx[None, :], 1.0 / HEAD_DIM, 0.0).astype(BF16)


def kernel(x_prompt, x_sample, cache_k, cache_v, c, c_ctx, mod_w, mod_b, norm_g, ffn1_wi, ffn1_wo, ffn2_wi, ffn2_wo, w_in, w_out, pool_w, pool_scale, conv_dw, conv_b, conv_norm_g, conv_pw, q_norm_g, k_norm_g, sink):
    batch, seq, _ = x_prompt.shape
    dec_batch, dec_seq, _ = x_sample.shape
    past = cache_k.shape[2]
    assert 1 + dec_batch <= COND_ROWS

    cond = jnp.zeros((COND_ROWS, D), F32).at[0].set(c_ctx).at[1:1 + dec_batch].set(c)
    mods = _modulations(cond, mod_w, mod_b).reshape(DEPTH, COND_ROWS, 1, N_MOD * D)

    norm_g4 = norm_g.reshape(DEPTH, 3, 1, D)
    wi1, wo1, wi2, wo2, w_in_b, w_out_b = (
        w[0].astype(BF16) for w in (ffn1_wi, ffn1_wo, ffn2_wi, ffn2_wo, w_in, w_out))
    eye = jnp.eye(POOL_GROUPS, dtype=F32)
    pool_bd = (eye[None, :, None, :, None] * pool_w[:, :, :, None, :]).reshape(
        DEPTH, POOL_WIDTH, POOL_WIDTH).astype(BF16)
    wts = {
        "pool_bd": pool_bd,
        "pool_scale": pool_scale.reshape(DEPTH, 1, POOL_WIDTH),
        "conv_dw": conv_dw,
        "conv_b": conv_b.reshape(DEPTH, 1, CONV_WIDTH),
        "conv_norm_g": conv_norm_g.reshape(DEPTH, 1, CONV_WIDTH),
        "conv_pw": conv_pw.astype(BF16),
        "sink": sink.reshape(DEPTH * N_HEADS),
    }
    mq, mk = _block_diag_mean(ATTN_WIDTH), _block_diag_mean(KV_WIDTH)
    qg = jnp.tile(q_norm_g, (1, N_HEADS)).reshape(DEPTH, 1, ATTN_WIDTH) * (SCALE * LOG2E)
    kg = jnp.tile(k_norm_g, (1, N_KV_HEADS)).reshape(DEPTH, 1, KV_WIDTH)
    rope = _rope_tables(dec_seq)
    cache_k4 = cache_k.reshape(dec_batch, DEPTH, past, KV_WIDTH)
    cache_v4 = cache_v.reshape(dec_batch, DEPTH, past, KV_WIDTH)

    def seqs(a, bsz, n):
        return a.reshape(bsz, n, a.shape[-1])

    yp = x_prompt.reshape(batch * seq, D)
    ys = x_sample.reshape(dec_batch * dec_seq, D)
    ks, vs = [], []
    for layer in range(DEPTH):
        def nxt(*ws):
            return [(w, layer + 1) for w in ws] if layer + 1 < DEPTH else []

        yp, *wi1_n = _ffn(yp, mods, norm_g4, wi1, wo1, layer, 0, None, 0, casts=nxt(ffn1_wi))
        c_parts = _proj(yp, mods, norm_g4, w_in_b, mq, mk, qg, kg, layer, None, 0)
        ys, mix_c, *wo1_n = _ffn_with_context_mixer(
            ys, mods, norm_g4, wi1, wo1, layer, 0, dec_seq, 1,
            *(seqs(a, batch, seq) for a in c_parts), wts, casts=nxt(ffn1_wo))
        l_parts = _proj(ys, mods, norm_g4, w_in_b, mq, mk, qg, kg, layer, dec_seq, 1,
                        rope_tables=rope)
        yp, *wi2_n = _ffn(yp, mods, norm_g4, wi2, wo2, layer, 2, None, 0,
                          mix=mix_c.reshape(batch * seq, MIX_WIDTH), w_out=w_out_b,
                          casts=nxt(ffn2_wi))
        mix_l = _mixer(*(seqs(a, dec_batch, dec_seq) for a in l_parts), wts, layer,
                       cache=(cache_k4, cache_v4))
        ys, *rest_n = _ffn(ys, mods, norm_g4, wi2, wo2, layer, 2, dec_seq, 1,
                           mix=mix_l.reshape(dec_batch * dec_seq, MIX_WIDTH), w_out=w_out_b,
                           casts=nxt(ffn2_wo, w_in, w_out))
        ks.append(c_parts[3].reshape(batch, seq, N_KV_HEADS, HEAD_DIM))
        vs.append(c_parts[4].reshape(batch, seq, N_KV_HEADS, HEAD_DIM))
        if layer + 1 < DEPTH:
            (wi1,), (wo1,), (wi2,), (wo2, w_in_b, w_out_b) = wi1_n, wo1_n, wi2_n, rest_n
    return (yp.reshape(batch, seq, D), ys.reshape(dec_batch, dec_seq, D),
            jnp.stack(ks, axis=1), jnp.stack(vs, axis=1))
```

```python
import functools

import jax
import jax.numpy as jnp
import numpy as np
from jax import lax
from jax.experimental import pallas as pl
from jax.experimental.pallas import tpu as pltpu

D = 1024
DEPTH = 2
GRID_W = 64
POOL_WIDTH = 256
POOL_GROUPS = 4
POOL_GROUP_DIM = POOL_WIDTH // POOL_GROUPS
CONV_WIDTH = 256
CONV_TAPS = 31
N_HEADS = 8
N_KV_HEADS = 2
HEAD_DIM = 64
Q_GROUP = N_HEADS // N_KV_HEADS
ATTN_WIDTH = N_HEADS * HEAD_DIM
KV_WIDTH = N_KV_HEADS * HEAD_DIM
MIX_WIDTH = POOL_WIDTH + CONV_WIDTH + ATTN_WIDTH
ATTN_OFFSET = POOL_WIDTH + 2 * CONV_WIDTH
IN_WIDTH = ATTN_OFFSET + ATTN_WIDTH + 2 * KV_WIDTH
BLOCK = 128
D_FF = 2816
N_MOD = 9
ROPE_THETA = 10000.0
EPS = 1e-6
NEG_INF = -1e30
SCALE = HEAD_DIM ** -0.5
LOG2E = 1.4426950408889634

COND_ROWS = 8
HALO = 16
TM_FFN = 1024
FFN_SUB = 512
FFN_PIECE = 128
TM_PROJ = 512
MXU_TILE = 256
FFN_CHUNK = 2 * MXU_TILE
TB = 256
ROW_CHUNK = 64
ZROWS = TB + 24
VMEM_LIMIT = 56 * 1024 * 1024

F32 = jnp.float32
BF16 = jnp.bfloat16


def _dot(a, b):
    return jnp.dot(a, b, preferred_element_type=F32)


def _rms_mod(x, g, sc, sh):
    ms = jnp.mean(x * x, axis=-1, keepdims=True)
    return (x * lax.rsqrt(ms + EPS) * g) * (1.0 + sc) + sh


def _mod_kernel(cond_ref, w_ref, b_ref, o_ref):
    c = cond_ref[...]
    s = (c * jax.nn.sigmoid(c)).astype(BF16)
    o_ref[...] = _dot(s, w_ref[...].astype(BF16)) + b_ref[...]


def _modulations(cond, mod_w, mod_b):
    tn = 1024
    nt = (N_MOD * D) // tn
    return pl.pallas_call(
        _mod_kernel,
        grid=(DEPTH, nt),
        in_specs=[
            pl.BlockSpec((COND_ROWS, D), lambda l, j: (0, 0)),
            pl.BlockSpec((None, D, tn), lambda l, j: (l, 0, j)),
            pl.BlockSpec((None, 1, tn), lambda l, j: (l, 0, j)),
        ],
        out_specs=pl.BlockSpec((None, COND_ROWS, tn), lambda l, j: (l, 0, j)),
        out_shape=jax.ShapeDtypeStruct((DEPTH, COND_ROWS, N_MOD * D), F32),
        compiler_params=pltpu.CompilerParams(
            dimension_semantics=("arbitrary", "arbitrary"),
            vmem_limit_bytes=VMEM_LIMIT),
        name="modulations",
    )(cond, mod_w, mod_b.reshape(DEPTH, 1, N_MOD * D))


def _cond_row(i, tm, tokens_per_cond, first_row):
    if tokens_per_cond is None:
        return first_row
    return first_row + i // (tokens_per_cond // tm)


def _mod_spec(layer, which, tm, tokens_per_cond, first_row):
    return pl.BlockSpec(
        (None, None, 1, 3 * D),
        lambda i: (layer, _cond_row(i, tm, tokens_per_cond, first_row), 0, which))


def _const_spec(shape, index):
    return pl.BlockSpec(shape, lambda i: index, pipeline_mode=pl.Buffered(1))


def _ffn_kernel(*refs, pre_outproj, side=None):
    if pre_outproj:
        (x_ref, mix_ref, wout_ref, modp_ref, mod_ref, g_ref, wi_ref, wo_ref,
         o_ref, a_ref, h_ref) = refs
        x1_ref = o_ref
    else:
        x_ref, mod_ref, g_ref, wi_ref, wo_ref, o_ref, a_ref, h_ref = refs
        x1_ref = x_ref
    n_sub = x_ref.shape[0] // FFN_SUB

    def rows(s):
        return slice(s * FFN_SUB, (s + 1) * FFN_SUB)

    def outproj(s):
        if pre_outproj:
            x1_ref[rows(s), :] = x_ref[rows(s), :] + modp_ref[:, 2 * D:3 * D] * _dot(
                mix_ref[rows(s), :], wout_ref[...])

    def prologue(s, piece):
        r0 = s * FFN_SUB + piece * FFN_PIECE
        p0 = piece * FFN_PIECE
        x = x1_ref[r0:r0 + FFN_PIECE, :]
        h = _rms_mod(x, g_ref[...], mod_ref[:, D:2 * D], mod_ref[:, 0:D]).astype(BF16)
        h_ref[s, p0:p0 + FFN_PIECE, :] = h
        token = h[0:16, :]
        for r in range(16, FFN_PIECE, 16):
            token = token + h[r:r + 16, :]
        return functools.reduce(lambda a, b: a + b, [token[:, l:l + 128] for l in range(0, D, 128)])

    def anchor(s, token):
        zero = jnp.zeros_like(token)
        h_ref[s, 0:16, 0:128] = h_ref[s, 0:16, 0:128] + jnp.maximum(jnp.minimum(token, zero), zero)

    n_piece = FFN_SUB // FFN_PIECE
    outproj(0)
    for piece in range(n_piece):
        prologue(0, piece)
    n_chunk = -(-D_FF // FFN_CHUNK)
    for s in range(n_sub):
        items = [] if side is None else side[s]
        per_chunk = -(-len(items) // n_chunk)
        if s + 1 < n_sub:
            outproj(s + 1)
        for c, c0 in enumerate(range(0, D_FF, FFN_CHUNK)):
            c1 = min(c0 + FFN_CHUNK, D_FF)
            gate = _dot(h_ref[s], wi_ref[:, c0:c1])
            up = _dot(h_ref[s], wi_ref[:, D_FF + c0:D_FF + c1])
            a_ref[s, :, c0:c1] = (gate * jax.nn.sigmoid(gate) * up).astype(BF16)
            if s + 1 < n_sub and c < n_piece:
                anchor(s, prologue(s + 1, c))
            for item in items[c * per_chunk:(c + 1) * per_chunk]:
                item()
        o_ref[rows(s), :] = x1_ref[rows(s), :] + (0.5 * mod_ref[:, 2 * D:3 * D]) * _dot(
            a_ref[s], wo_ref[...])


def _with_casts(body, n_in, n_out, n_cast):
    if n_cast == 0:
        return body

    def kernel(*refs):
        ins, srcs = refs[:n_in], refs[n_in:n_in + n_cast]
        outs = refs[n_in + n_cast:n_in + n_cast + n_out]
        dsts = refs[n_in + n_cast + n_out:n_in + 2 * n_cast + n_out]
        for src, dst in zip(srcs, dsts):
            dst[...] = src[...].astype(BF16)
        body(*ins, *outs, *refs[n_in + 2 * n_cast + n_out:])

    return kernel


def _cast_specs(casts, steps):
    in_specs, out_specs, out_shapes, args = [], [], [], []
    for w, layer in casts:
        _, r, c = w.shape
        rows = r // steps
        assert r % steps == 0 and rows % 16 == 0
        in_specs.append(pl.BlockSpec((None, rows, c), lambda i, layer=layer: (layer, i, 0)))
        out_specs.append(pl.BlockSpec((rows, c), lambda i: (i, 0)))
        out_shapes.append(jax.ShapeDtypeStruct((r, c), BF16))
        args.append(w)
    return in_specs, out_specs, out_shapes, args


def _ffn(x, mods, norm_g, wi, wo, layer, which, tokens_per_cond, first_row,
         mix=None, w_out=None, casts=()):
    t = x.shape[0]
    tm = TM_FFN
    tok = pl.BlockSpec((tm, D), lambda i: (i, 0))
    in_specs = [tok]
    args = [x]
    if mix is not None:
        in_specs += [tok, _const_spec((MIX_WIDTH, D), (0, 0)),
                     _mod_spec(layer, 1, tm, tokens_per_cond, first_row)]
        args += [mix, w_out, mods]
    in_specs += [
        _mod_spec(layer, which, tm, tokens_per_cond, first_row),
        _const_spec((None, None, 1, D), (layer, which, 0, 0)),
        _const_spec((D, 2 * D_FF), (0, 0)),
        _const_spec((D_FF, D), (0, 0)),
    ]
    args += [mods, norm_g, wi, wo]
    c_in, c_out, c_shapes, c_args = _cast_specs(casts, t // tm)
    return pl.pallas_call(
        _with_casts(functools.partial(_ffn_kernel, pre_outproj=mix is not None),
                    len(in_specs), 1, len(casts)),
        grid=(t // tm,),
        in_specs=in_specs + c_in,
        out_specs=[tok] + c_out,
        out_shape=[jax.ShapeDtypeStruct((t, D), F32)] + c_shapes,
        scratch_shapes=[pltpu.VMEM((tm // FFN_SUB, FFN_SUB, D_FF), BF16),
                        pltpu.VMEM((tm // FFN_SUB, FFN_SUB, D), BF16)],
        compiler_params=pltpu.CompilerParams(
            dimension_semantics=("arbitrary",), vmem_limit_bytes=VMEM_LIMIT),
        name="ffn_outproj" if mix is not None else "ffn",
    )(*args, *c_args)


def _head_norm(x, m, g):
    ms = _dot((x * x).astype(BF16), m)
    return x * lax.rsqrt(ms + EPS) * g


def _rope(x, c, s_up, s_dn):
    n = x.shape[1]
    return x * c + pltpu.roll(x, n - 16, 1) * s_up + pltpu.roll(x, 16, 1) * s_dn


def _proj_kernel(*refs, rope):
    if rope:
        (x_ref, mod_ref, g_ref, win_ref, mq_ref, mk_ref, qg_ref, kg_ref,
         c_ref, su_ref, sd_ref, pool_ref, glu_ref, q_ref, k_ref, v_ref) = refs
    else:
        (x_ref, mod_ref, g_ref, win_ref, mq_ref, mk_ref, qg_ref, kg_ref,
         pool_ref, glu_ref, q_ref, k_ref, v_ref) = refs
    h = _rms_mod(x_ref[...], g_ref[...], mod_ref[:, D:2 * D], mod_ref[:, 0:D]).astype(BF16)
    u = _dot(h, win_ref[...])
    pool_ref[...] = u[:, :POOL_WIDTH]
    glu_ref[...] = (u[:, POOL_WIDTH:POOL_WIDTH + CONV_WIDTH]
                    * jax.nn.sigmoid(u[:, POOL_WIDTH + CONV_WIDTH:ATTN_OFFSET]))
    q = _head_norm(u[:, ATTN_OFFSET:ATTN_OFFSET + ATTN_WIDTH], mq_ref[...], qg_ref[...])
    k = _head_norm(u[:, ATTN_OFFSET + ATTN_WIDTH:ATTN_OFFSET + ATTN_WIDTH + KV_WIDTH],
                   mk_ref[...], kg_ref[...])
    if rope:
        c, su, sd = c_ref[...], su_ref[...], sd_ref[...]
        k = _rope(k, c, su, sd)
        rep = ATTN_WIDTH // KV_WIDTH
        q = _rope(q, jnp.concatenate([c] * rep, axis=1), jnp.concatenate([su] * rep, axis=1),
                  jnp.concatenate([sd] * rep, axis=1))
    q_ref[...] = q.astype(BF16)
    k_ref[...] = k
    v_ref[...] = u[:, ATTN_OFFSET + ATTN_WIDTH + KV_WIDTH:]


def _proj(x, mods, norm_g, w_in, mq, mk, qg, kg, layer, tokens_per_cond, first_row,
          rope_tables=None, casts=()):
    t = x.shape[0]
    tm = TM_PROJ
    in_specs = [
        pl.BlockSpec((tm, D), lambda i: (i, 0)),
        _mod_spec(layer, 1, tm, tokens_per_cond, first_row),
        _const_spec((None, None, 1, D), (layer, 1, 0, 0)),
        _const_spec((D, IN_WIDTH), (0, 0)),
        _const_spec((ATTN_WIDTH, ATTN_WIDTH), (0, 0)),
        _const_spec((KV_WIDTH, KV_WIDTH), (0, 0)),
        _const_spec((None, 1, ATTN_WIDTH), (layer, 0, 0)),
        _const_spec((None, 1, KV_WIDTH), (layer, 0, 0)),
    ]
    args = [x, mods, norm_g, w_in, mq, mk, qg, kg]
    if rope_tables is not None:
        seq_tiles = rope_tables[0].shape[0] // tm
        in_specs += [pl.BlockSpec((tm, KV_WIDTH), lambda i: (i % seq_tiles, 0))] * 3
        args += list(rope_tables)

    def out(width):
        return pl.BlockSpec((tm, width), lambda i: (i, 0))

    c_in, c_out, c_shapes, c_args = _cast_specs(casts, t // tm)
    out_specs = [out(POOL_WIDTH), out(CONV_WIDTH), out(ATTN_WIDTH), out(KV_WIDTH), out(KV_WIDTH)]
    return pl.pallas_call(
        _with_casts(functools.partial(_proj_kernel, rope=rope_tables is not None),
                    len(in_specs), len(out_specs), len(casts)),
        grid=(t // tm,),
        in_specs=in_specs + c_in,
        out_specs=out_specs + c_out,
        out_shape=[
            jax.ShapeDtypeStruct((t, POOL_WIDTH), F32),
            jax.ShapeDtypeStruct((t, CONV_WIDTH), F32),
            jax.ShapeDtypeStruct((t, ATTN_WIDTH), BF16),
            jax.ShapeDtypeStruct((t, KV_WIDTH), F32),
            jax.ShapeDtypeStruct((t, KV_WIDTH), F32),
        ] + c_shapes,
        compiler_params=pltpu.CompilerParams(
            dimension_semantics=("arbitrary",), vmem_limit_bytes=VMEM_LIMIT),
        name="proj_rope" if rope_tables is not None else "proj",
    )(*args, *c_args)


def _fill_ext(ext_ref, prev_ref, cur_ref, next_ref, i, nb):
    ext_ref[0:HALO, :] = jnp.where(i > 0, prev_ref[...], 0.0)
    ext_ref[HALO:HALO + TB, :] = cur_ref[...]
    ext_ref[HALO + TB:, :] = jnp.where(i < nb - 1, next_ref[...], 0.0)


def _shift_copies(ext_ref, z_ref, shifts, lanes):
    for b in shifts:
        z_ref[b, :, lanes] = ext_ref[b:b + ZROWS, lanes]


def _window(z_ref, ext_off, r0, lanes):
    a, b = divmod(ext_off, 8)
    return z_ref[b, 8 * a + r0:8 * a + r0 + ROW_CHUNK, lanes]


def _pool_chunk(z_ref, pooled_ref, r0, i, n):
    lo_lanes, hi_lanes = slice(0, 128), slice(128, 256)
    first = lax.broadcasted_iota(jnp.int32, (ROW_CHUNK, 128), 1) < POOL_GROUP_DIM
    t = i * TB + r0 + lax.broadcasted_iota(jnp.int32, (ROW_CHUNK, 128), 0)

    def centred(total, half, cur):
        count = jnp.minimum(t + half, n) - jnp.maximum(t - half, 0)
        return (total / count.astype(F32) - cur).astype(BF16)

    def u(off, lanes):
        return _window(z_ref, HALO + off, r0, lanes)

    cur = u(0, lo_lanes)
    acc2 = u(-1, lo_lanes) + cur
    acc4 = acc2 + u(-2, lo_lanes) + u(1, lo_lanes)
    pooled_ref[r0:r0 + ROW_CHUNK, lo_lanes] = centred(
        jnp.where(first, acc2, acc4), jnp.where(first, 1, 2), cur)
    cur = u(0, hi_lanes)
    acc8 = cur
    for off in (-4, -3, -2, -1, 1, 2, 3):
        acc8 = acc8 + u(off, hi_lanes)
    acc16 = acc8
    for off in (-8, -7, -6, -5, 4, 5, 6, 7):
        acc16 = acc16 + u(off, hi_lanes)
    pooled_ref[r0:r0 + ROW_CHUNK, hi_lanes] = centred(
        jnp.where(first, acc8, acc16), jnp.where(first, 4, 8), cur)


def _conv_chunk(z_ref, act_ref, r0, dw_ref, cb_ref, cg_ref):
    acc = None
    for k in range(CONV_TAPS):
        term = _window(z_ref, HALO - CONV_TAPS // 2 + k, r0, slice(None)) * dw_ref[k:k + 1, :]
        acc = term if acc is None else acc + term
    y = acc + cb_ref[...]
    ms = jnp.mean(y * y, axis=-1, keepdims=True)
    z = y * lax.rsqrt(ms + EPS) * cg_ref[...]
    act_ref[r0:r0 + ROW_CHUNK, :] = (z * jax.nn.sigmoid(z)).astype(BF16)


def _att_scores(s_ref, q_ref, q0, j, kb, biases):
    lane = lax.broadcasted_iota(jnp.int32, (BLOCK, KV_WIDTH), 1)
    own = (lane // HEAD_DIM) == j
    qs = []
    for h in range(j * Q_GROUP, (j + 1) * Q_GROUP):
        c0 = (h // 2) * KV_WIDTH
        qh = q_ref[q0:q0 + BLOCK, c0:c0 + KV_WIDTH]
        if h % 2 != j:
            qh = jnp.concatenate([qh[:, HEAD_DIM:], qh[:, :HEAD_DIM]], axis=1)
        qs.append(jnp.where(own, qh, jnp.zeros_like(qh)))
    qj = jnp.concatenate(qs, axis=0)
    s = lax.dot_general(kb, qj, (((1,), (1,)), ((), ())), preferred_element_type=F32)
    if biases is None:
        s_ref[...] = s
    else:
        s_ref[0:BLOCK] = s[0:BLOCK] + biases[0]
        s_ref[BLOCK:2 * BLOCK] = s[BLOCK:2 * BLOCK]
        s_ref[2 * BLOCK:3 * BLOCK] = s[2 * BLOCK:3 * BLOCK] + biases[1]
        s_ref[3 * BLOCK:] = s[3 * BLOCK:]


def _att_weights(p_ref, s_ref, j, sink_ref, layer):
    sink = jnp.concatenate(
        [jnp.full((1, BLOCK), sink_ref[layer * N_HEADS + j * Q_GROUP + g] * LOG2E, F32)
         for g in range(Q_GROUP)], axis=1)
    m = jnp.maximum(jnp.max(s_ref[...], axis=0, keepdims=True), sink)
    for r in range(0, s_ref.shape[0], BLOCK):
        p_ref[r:r + BLOCK] = jnp.exp2(s_ref[r:r + BLOCK] - m).astype(BF16)
    return jnp.exp2(sink - m)


def _att_values(p_ref, sink_w, j, vt):
    v_ones = jnp.concatenate(
        [vt[j * HEAD_DIM:(j + 1) * HEAD_DIM, :], jnp.ones((16, vt.shape[1]), BF16)], axis=0)
    o = _dot(v_ones, p_ref[...])
    return o[0:HEAD_DIM, :] / (o[HEAD_DIM:HEAD_DIM + 1, :] + sink_w)


N_ATT_STAGES = (TB // BLOCK) * N_KV_HEADS + 2


def _mixer_parts(refs, layer, latent, n, i):
    nb = n // TB
    (pp_ref, pc_ref, pn_ref, gp_ref, gc_ref, gn_ref, q_ref) = refs[:7]
    rest = refs[7:]
    if latent:
        (kp_ref, kc_ref, kn_ref, vp_ref, vc_ref, vn_ref, ck_ref, cv_ref) = rest[:8]
        rest = rest[8:]
    else:
        (ka_ref, va_ref) = rest[:2]
        rest = rest[2:]
    (poolw_ref, pscale_ref, dw_ref, cb_ref, cg_ref, pw_ref, sink_ref,
     o_ref, pext_ref, gext_ref, zp_ref, zg_ref, pooled_ref, act_ref,
     s_ref, p_ref, at_ref) = rest

    operands = []

    def prep():
        _fill_ext(pext_ref, pp_ref, pc_ref, pn_ref, i, nb)
        _shift_copies(pext_ref, zp_ref, (6, 7, 0, 1), slice(0, 128))
        _shift_copies(pext_ref, zp_ref, range(8), slice(128, 256))
        _fill_ext(gext_ref, gp_ref, gc_ref, gn_ref, i, nb)
        _shift_copies(gext_ref, zg_ref, range(8), slice(None))
        if latent:
            k_seq = jnp.concatenate([kp_ref[...], kc_ref[...], kn_ref[...]], axis=0).astype(BF16)
            vt_seq = jnp.concatenate(
                [vp_ref[...], vc_ref[...], vn_ref[...]], axis=0).T.astype(BF16)
            k_ctx, vt_ctx = ck_ref[...].astype(BF16), cv_ref[...].T.astype(BF16)
            cols = Q_GROUP * BLOCK
            key = lax.broadcasted_iota(jnp.int32, (BLOCK, cols), 0)
            r = lax.broadcasted_iota(jnp.int32, (BLOCK, cols), 1) % BLOCK
        else:
            k_all, vt_all = ka_ref[...].astype(BF16), va_ref[...].T.astype(BF16)
        for qb in range(TB // BLOCK):
            q0 = qb * BLOCK
            biases = None
            if latent:
                k_all = jnp.concatenate([k_seq[q0:q0 + 3 * BLOCK], k_ctx], axis=0)
                vt_all = jnp.concatenate([vt_seq[:, q0:q0 + 3 * BLOCK], vt_ctx], axis=1)
                blk = i * (TB // BLOCK) + qb
                biases = (jnp.where((key >= r) & (blk > 0), 0.0, NEG_INF).astype(F32),
                          jnp.where((key <= r) & (blk < n // BLOCK - 1), 0.0, NEG_INF).astype(F32))
            operands.append((q0, k_all, vt_all, biases))

    vpu_chunks = []
    for r0 in range(0, TB, ROW_CHUNK):
        vpu_chunks.append(functools.partial(_pool_chunk, zp_ref, pooled_ref, r0, i, n))
        vpu_chunks.append(functools.partial(_conv_chunk, zg_ref, act_ref, r0, dw_ref, cb_ref, cg_ref))

    def pool_conv_matmuls():
        o_ref[:, 0:POOL_WIDTH] = (
            _dot(pooled_ref[...], poolw_ref[...]) * pscale_ref[...]).astype(BF16)
        o_ref[:, POOL_WIDTH:POOL_WIDTH + CONV_WIDTH] = _dot(act_ref[...], pw_ref[...]).astype(BF16)

    units = [(qb, j) for qb in range(TB // BLOCK) for j in range(N_KV_HEADS)]
    sink_w = {}

    def att_stage(t):
        if t < len(units):
            qb, j = units[t]
            q0, k_all, _, biases = operands[qb]
            _att_scores(s_ref.at[t % 2], q_ref, q0, j, k_all, biases)
        if 0 <= t - 1 < len(units):
            u = t - 1
            sink_w[u] = _att_weights(p_ref.at[u % 2], s_ref.at[u % 2], units[u][1], sink_ref, layer)
        if 0 <= t - 2 < len(units):
            u = t - 2
            qb, j = units[u]
            o = _att_values(p_ref.at[u % 2], sink_w.pop(u), j, operands[qb][2])
            for g in range(Q_GROUP):
                c0 = (j * Q_GROUP + g) * HEAD_DIM
                at_ref[qb, c0:c0 + HEAD_DIM, :] = o[:, g * BLOCK:(g + 1) * BLOCK]

    def att_store():
        for qb in range(TB // BLOCK):
            o_ref[qb * BLOCK:(qb + 1) * BLOCK, POOL_WIDTH + CONV_WIDTH:] = (
                at_ref[qb].T.astype(BF16))

    return prep, att_stage, vpu_chunks, pool_conv_matmuls, att_store


def _mixer_items(parts):
    prep, att_stage, vpu_chunks, pool_conv_matmuls, att_store = parts
    per_stage = -(-len(vpu_chunks) // N_ATT_STAGES)

    def stage(t):
        att_stage(t)
        for chunk in vpu_chunks[t * per_stage:(t + 1) * per_stage]:
            chunk()

    def finish():
        pool_conv_matmuls()
        att_store()

    return [prep] + [functools.partial(stage, t) for t in range(N_ATT_STAGES)] + [finish]


def _mixer_kernel(*refs, layer, latent, n):
    prep, att_stage, vpu_chunks, pool_conv_matmuls, att_store = _mixer_parts(
        refs, layer, latent, n, pl.program_id(1))
    prep()
    if latent:
        for chunk in vpu_chunks:
            chunk()
        pool_conv_matmuls()
        for t in range(N_ATT_STAGES):
            att_stage(t)
    else:
        per_stage = -(-len(vpu_chunks) // N_ATT_STAGES)
        for t in range(N_ATT_STAGES):
            att_stage(t)
            for chunk in vpu_chunks[t * per_stage:(t + 1) * per_stage]:
                chunk()
        pool_conv_matmuls()
    att_store()


def _mixer(pool_in, glu, q, k, v, wts, layer, cache=None):
    b, n, _ = pool_in.shape
    nb = n // TB
    hpb = TB // HALO
    nh = n // HALO
    kpb = TB // BLOCK
    nk = n // BLOCK
    n_keys = n if cache is None else 3 * BLOCK + cache[0].shape[2]

    def cur(width):
        return pl.BlockSpec((None, TB, width), lambda bi, i: (bi, i, 0))

    def halo_prev(width):
        return pl.BlockSpec((None, HALO, width), lambda bi, i: (bi, jnp.maximum(i * hpb - 1, 0), 0))

    def halo_next(width):
        return pl.BlockSpec((None, HALO, width),
                            lambda bi, i: (bi, jnp.minimum((i + 1) * hpb, nh - 1), 0))

    in_specs = [halo_prev(POOL_WIDTH), cur(POOL_WIDTH), halo_next(POOL_WIDTH),
                halo_prev(CONV_WIDTH), cur(CONV_WIDTH), halo_next(CONV_WIDTH),
                cur(ATTN_WIDTH)]
    args = [pool_in, pool_in, pool_in, glu, glu, glu, q]
    if cache is not None:
        cache_k, cache_v = cache
        past = cache_k.shape[2]
        blk_prev = pl.BlockSpec((None, BLOCK, KV_WIDTH),
                                lambda bi, i: (bi, jnp.maximum(i * kpb - 1, 0), 0))
        blk_next = pl.BlockSpec((None, BLOCK, KV_WIDTH),
                                lambda bi, i: (bi, jnp.minimum((i + 1) * kpb, nk - 1), 0))
        cspec = pl.BlockSpec((None, None, past, KV_WIDTH), lambda bi, i: (bi, layer, 0, 0))
        in_specs += [blk_prev, cur(KV_WIDTH), blk_next, blk_prev, cur(KV_WIDTH), blk_next,
                     cspec, cspec]
        args += [k, k, k, v, v, v, cache_k, cache_v]
    else:
        whole = pl.BlockSpec((None, n, KV_WIDTH), lambda bi, i: (bi, 0, 0))
        in_specs += [whole, whole]
        args += [k, v]
    in_specs += _mixer_weight_specs(layer)
    args += _mixer_weight_args(wts)
    return pl.pallas_call(
        functools.partial(_mixer_kernel, layer=layer, latent=cache is not None, n=n),
        grid=(b, nb),
        in_specs=in_specs,
        out_specs=pl.BlockSpec((None, TB, MIX_WIDTH), lambda bi, i: (bi, i, 0)),
        out_shape=jax.ShapeDtypeStruct((b, n, MIX_WIDTH), BF16),
        scratch_shapes=_mixer_scratch(n_keys),
        compiler_params=pltpu.CompilerParams(
            dimension_semantics=("arbitrary", "arbitrary"), vmem_limit_bytes=VMEM_LIMIT),
        name="mixer_latent" if cache is not None else "mixer_context",
    )(*args)


def _mixer_weight_specs(layer):
    def const(shape, index):
        return pl.BlockSpec(shape, lambda *_: index)

    return [
        const((None, POOL_WIDTH, POOL_WIDTH), (layer, 0, 0)),
        const((None, 1, POOL_WIDTH), (layer, 0, 0)),
        const((None, CONV_TAPS, CONV_WIDTH), (layer, 0, 0)),
        const((None, 1, CONV_WIDTH), (layer, 0, 0)),
        const((None, 1, CONV_WIDTH), (layer, 0, 0)),
        const((None, CONV_WIDTH, CONV_WIDTH), (layer, 0, 0)),
        pl.BlockSpec(memory_space=pltpu.SMEM),
    ]


def _mixer_weight_args(wts):
    return [wts["pool_bd"], wts["pool_scale"], wts["conv_dw"], wts["conv_b"],
            wts["conv_norm_g"], wts["conv_pw"], wts["sink"]]


def _mixer_scratch(n_keys):
    return [pltpu.VMEM((TB + 2 * HALO, POOL_WIDTH), F32),
            pltpu.VMEM((TB + 2 * HALO, CONV_WIDTH), F32),
            pltpu.VMEM((8, ZROWS, POOL_WIDTH), F32),
            pltpu.VMEM((8, ZROWS, CONV_WIDTH), F32),
            pltpu.VMEM((TB, POOL_WIDTH), BF16),
            pltpu.VMEM((TB, CONV_WIDTH), BF16),
            pltpu.VMEM((2, n_keys, Q_GROUP * BLOCK), F32),
            pltpu.VMEM((2, n_keys, Q_GROUP * BLOCK), BF16),
            pltpu.VMEM((TB // BLOCK, ATTN_WIDTH, BLOCK), F32)]


N_CTX_BLOCK_OPERANDS = 9


def _ffn_side_kernel(*refs, n_ffn_in, n_mix_in, n_ffn_scratch, layer, n):
    ffn_in = refs[:n_ffn_in]
    mix_in = refs[n_ffn_in:n_ffn_in + n_mix_in]
    o_ref, mix_out = refs[n_ffn_in + n_mix_in:n_ffn_in + n_mix_in + 2]
    scratch = refs[n_ffn_in + n_mix_in + 2:]
    ffn_scratch, mix_scratch = scratch[:n_ffn_scratch], scratch[n_ffn_scratch:]
    side = []
    for s in range(mix_out.shape[0]):
        block = (tuple(r.at[s] for r in mix_in[:N_CTX_BLOCK_OPERANDS])
                 + tuple(mix_in[N_CTX_BLOCK_OPERANDS:]) + (mix_out.at[s],) + tuple(mix_scratch))
        side.append(_mixer_items(_mixer_parts(block, layer, False, n, 0)))
    _ffn_kernel(*ffn_in, o_ref, *ffn_scratch, pre_outproj=False, side=side)


def _ffn_with_context_mixer(x, mods, norm_g, wi, wo, layer, which, tokens_per_cond, first_row,
                            pool_in, glu, q, k, v, wts, casts=()):
    t = x.shape[0]
    tm = TM_FFN
    b, n, _ = pool_in.shape
    per_step = tm // FFN_SUB
    assert n == TB and b == (t // tm) * per_step
    tok = pl.BlockSpec((tm, D), lambda i: (i, 0))
    ffn_specs = [
        tok,
        _mod_spec(layer, which, tm, tokens_per_cond, first_row),
        _const_spec((None, None, 1, D), (layer, which, 0, 0)),
        _const_spec((D, 2 * D_FF), (0, 0)),
        _const_spec((D_FF, D), (0, 0)),
    ]
    ffn_args = [x, mods, norm_g, wi, wo]

    def seqs(rows, width):
        return pl.BlockSpec((per_step, rows, width), lambda i: (i, 0, 0))

    mix_specs = [seqs(HALO, POOL_WIDTH), seqs(TB, POOL_WIDTH), seqs(HALO, POOL_WIDTH),
                 seqs(HALO, CONV_WIDTH), seqs(TB, CONV_WIDTH), seqs(HALO, CONV_WIDTH),
                 seqs(TB, ATTN_WIDTH), seqs(TB, KV_WIDTH), seqs(TB, KV_WIDTH)]
    mix_args = [pool_in, pool_in, pool_in, glu, glu, glu, q, k, v]
    assert len(mix_specs) == N_CTX_BLOCK_OPERANDS
    mix_specs += _mixer_weight_specs(layer)
    mix_args += _mixer_weight_args(wts)
    ffn_scratch = [pltpu.VMEM((tm // FFN_SUB, FFN_SUB, D_FF), BF16),
                   pltpu.VMEM((tm // FFN_SUB, FFN_SUB, D), BF16)]
    c_in, c_out, c_shapes, c_args = _cast_specs(casts, t // tm)
    body = functools.partial(_ffn_side_kernel, n_ffn_in=len(ffn_specs), n_mix_in=len(mix_specs),
                             n_ffn_scratch=len(ffn_scratch), layer=layer, n=n)
    return pl.pallas_call(
        _with_casts(body, len(ffn_specs) + len(mix_specs), 2, len(casts)),
        grid=(t // tm,),
        in_specs=ffn_specs + mix_specs + c_in,
        out_specs=[tok, pl.BlockSpec((per_step, TB, MIX_WIDTH), lambda i: (i, 0, 0))] + c_out,
        out_shape=[jax.ShapeDtypeStruct((t, D), F32),
                   jax.ShapeDtypeStruct((b, n, MIX_WIDTH), BF16)] + c_shapes,
        scratch_shapes=ffn_scratch + _mixer_scratch(n),
        compiler_params=pltpu.CompilerParams(
            dimension_semantics=("arbitrary",), vmem_limit_bytes=VMEM_LIMIT),
        name="ffn_with_context_mixer",
    )(*ffn_args, *mix_args, *c_args)


def _rope_tables(n):
    t = np.arange(n)
    half = HEAD_DIM // 2
    inv = ROPE_THETA ** (-np.arange(0, half, 2, dtype=np.float64) / half)
    zeros = np.zeros((n, half // 2))
    cos, s_up, s_dn = [], [], []
    for pos in (t // GRID_W, t % GRID_W):
        ang = pos[:, None].astype(np.float64) * inv[None, :]
        cos += [np.cos(ang), np.cos(ang)]
        s_up += [-np.sin(ang), zeros]
        s_dn += [zeros, np.sin(ang)]
    return tuple(jnp.asarray(np.concatenate(parts * N_KV_HEADS, axis=1), dtype=F32)
                 for parts in (cos, s_up, s_dn))


def _block_diag_mean(width):
    idx = jnp.arange(width) // HEAD_DIM
    return jnp.where(idx[:, None] == idx[None, :], 1.0 / HEAD_DIM, 0.0).astype(BF16)


def kernel(x_prompt, x_sample, cache_k, cache_v, c, c_ctx, mod_w, mod_b, norm_g, ffn1_wi, ffn1_wo, ffn2_wi, ffn2_wo, w_in, w_out, pool_w, pool_scale, conv_dw, conv_b, conv_norm_g, conv_pw, q_norm_g, k_norm_g, sink):
    batch, seq, _ = x_prompt.shape
    dec_batch, dec_seq, _ = x_sample.shape
    past = cache_k.shape[2]
    assert 1 + dec_batch <= COND_ROWS

    cond = jnp.zeros((COND_ROWS, D), F32).at[0].set(c_ctx).at[1:1 + dec_batch].set(c)
    mods = _modulations(cond, mod_w, mod_b).reshape(DEPTH, COND_ROWS, 1, N_MOD * D)

    norm_g4 = norm_g.reshape(DEPTH, 3, 1, D)
    wi1, wo1 = ffn1_wi[0].astype(BF16), ffn1_wo[0].astype(BF16)
    wi2 = wo2 = w_in_b = w_out_b = None
    eye = jnp.eye(POOL_GROUPS, dtype=F32)
    pool_bd = (eye[None, :, None, :, None] * pool_w[:, :, :, None, :]).reshape(
        DEPTH, POOL_WIDTH, POOL_WIDTH).astype(BF16)
    wts = {
        "pool_bd": pool_bd,
        "pool_scale": pool_scale.reshape(DEPTH, 1, POOL_WIDTH),
        "conv_dw": conv_dw,
        "conv_b": conv_b.reshape(DEPTH, 1, CONV_WIDTH),
        "conv_norm_g": conv_norm_g.reshape(DEPTH, 1, CONV_WIDTH),
        "conv_pw": conv_pw.astype(BF16),
        "sink": sink.reshape(DEPTH * N_HEADS),
    }
    mq, mk = _block_diag_mean(ATTN_WIDTH), _block_diag_mean(KV_WIDTH)
    qg = jnp.tile(q_norm_g, (1, N_HEADS)).reshape(DEPTH, 1, ATTN_WIDTH) * (SCALE * LOG2E)
    kg = jnp.tile(k_norm_g, (1, N_KV_HEADS)).reshape(DEPTH, 1, KV_WIDTH)
    rope = _rope_tables(dec_seq)
    cache_k4 = cache_k.reshape(dec_batch, DEPTH, past, KV_WIDTH)
    cache_v4 = cache_v.reshape(dec_batch, DEPTH, past, KV_WIDTH)

    def seqs(a, bsz, n):
        return a.reshape(bsz, n, a.shape[-1])

    yp = x_prompt.reshape(batch * seq, D)
    ys = x_sample.reshape(dec_batch * dec_seq, D)
    ks, vs = [], []
    for layer in range(DEPTH):
        def nxt(*ws):
            return [(w, layer + 1) for w in ws] if layer + 1 < DEPTH else []

        def first(*ws):
            return [(w, 0) for w in ws] if layer == 0 else []

        yp, *now = _ffn(yp, mods, norm_g4, wi1, wo1, layer, 0, None, 0,
                        casts=first(w_in, w_out, ffn2_wo))
        if layer == 0:
            w_in_b, w_out_b, wo2 = now
        *c_parts, = _proj(yp, mods, norm_g4, w_in_b, mq, mk, qg, kg, layer, None, 0,
                          casts=first(ffn2_wi))
        if layer == 0:
            wi2 = c_parts.pop()
        ys, mix_c, *wo1_n = _ffn_with_context_mixer(
            ys, mods, norm_g4, wi1, wo1, layer, 0, dec_seq, 1,
            *(seqs(a, batch, seq) for a in c_parts), wts, casts=nxt(ffn1_wo))
        l_parts = _proj(ys, mods, norm_g4, w_in_b, mq, mk, qg, kg, layer, dec_seq, 1,
                        rope_tables=rope)
        yp, *wi2_n = _ffn(yp, mods, norm_g4, wi2, wo2, layer, 2, None, 0,
                          mix=mix_c.reshape(batch * seq, MIX_WIDTH), w_out=w_out_b,
                          casts=nxt(ffn2_wi))
        mix_l = _mixer(*(seqs(a, dec_batch, dec_seq) for a in l_parts), wts, layer,
                       cache=(cache_k4, cache_v4))
        ys, *rest_n = _ffn(ys, mods, norm_g4, wi2, wo2, layer, 2, dec_seq, 1,
                           mix=mix_l.reshape(dec_batch * dec_seq, MIX_WIDTH), w_out=w_out_b,
                           casts=nxt(ffn2_wo, w_in, w_out, ffn1_wi))
        ks.append(c_parts[3].reshape(batch, seq, N_KV_HEADS, HEAD_DIM))
        vs.append(c_parts[4].reshape(batch, seq, N_KV_HEADS, HEAD_DIM))
        if layer + 1 < DEPTH:
            (wo1,), (wi2,), (wo2, w_in_b, w_out_b, wi1) = wo1_n, wi2_n, rest_n
    return (yp.reshape(batch, seq, D), ys.reshape(dec_batch, dec_seq, D),
            jnp.stack(ks, axis=1), jnp.stack(vs, axis=1))
```

```python
import functools

import jax
import jax.numpy as jnp
import numpy as np
from jax import lax
from jax.experimental import pallas as pl
from jax.experimental.pallas import tpu as pltpu

D = 1024
DEPTH = 2
GRID_W = 64
POOL_WIDTH = 256
POOL_GROUPS = 4
POOL_GROUP_DIM = POOL_WIDTH // POOL_GROUPS
CONV_WIDTH = 256
CONV_TAPS = 31
N_HEADS = 8
N_KV_HEADS = 2
HEAD_DIM = 64
Q_GROUP = N_HEADS // N_KV_HEADS
ATTN_WIDTH = N_HEADS * HEAD_DIM
KV_WIDTH = N_KV_HEADS * HEAD_DIM
MIX_WIDTH = POOL_WIDTH + CONV_WIDTH + ATTN_WIDTH
ATTN_OFFSET = POOL_WIDTH + 2 * CONV_WIDTH
IN_WIDTH = ATTN_OFFSET + ATTN_WIDTH + 2 * KV_WIDTH
BLOCK = 128
D_FF = 2816
N_MOD = 9
ROPE_THETA = 10000.0
EPS = 1e-6
NEG_INF = -1e30
SCALE = HEAD_DIM ** -0.5
LOG2E = 1.4426950408889634

COND_ROWS = 8
HALO = 16
TM_FFN = 1024
FFN_SUB = 512
FFN_PIECE = 128
TM_PROJ = 1024
PROJ_SUB = 256
MXU_TILE = 256
FFN_CHUNK = 2 * MXU_TILE
TB = 256
ROW_CHUNK = 64
ZROWS = TB + 24
VMEM_LIMIT = 56 * 1024 * 1024

F32 = jnp.float32
BF16 = jnp.bfloat16


def _dot(a, b):
    return jnp.dot(a, b, preferred_element_type=F32)


def _rms_mod(x, g, sc, sh):
    ms = jnp.mean(x * x, axis=-1, keepdims=True)
    return (x * lax.rsqrt(ms + EPS) * g) * (1.0 + sc) + sh


def _mod_kernel(cond_ref, w_ref, b_ref, o_ref):
    c = cond_ref[...]
    s = (c * jax.nn.sigmoid(c)).astype(BF16)
    o_ref[...] = _dot(s, w_ref[...].astype(BF16)) + b_ref[...]


def _modulations(cond, mod_w, mod_b):
    tn = 1024
    nt = (N_MOD * D) // tn
    return pl.pallas_call(
        _mod_kernel,
        grid=(DEPTH, nt),
        in_specs=[
            pl.BlockSpec((COND_ROWS, D), lambda l, j: (0, 0)),
            pl.BlockSpec((None, D, tn), lambda l, j: (l, 0, j)),
            pl.BlockSpec((None, 1, tn), lambda l, j: (l, 0, j)),
        ],
        out_specs=pl.BlockSpec((None, COND_ROWS, tn), lambda l, j: (l, 0, j)),
        out_shape=jax.ShapeDtypeStruct((DEPTH, COND_ROWS, N_MOD * D), F32),
        compiler_params=pltpu.CompilerParams(
            dimension_semantics=("arbitrary", "arbitrary"),
            vmem_limit_bytes=VMEM_LIMIT),
        name="modulations",
    )(cond, mod_w, mod_b.reshape(DEPTH, 1, N_MOD * D))


def _cond_row(i, tm, tokens_per_cond, first_row):
    if tokens_per_cond is None:
        return first_row
    return first_row + i // (tokens_per_cond // tm)


def _mod_spec(layer, which, tm, tokens_per_cond, first_row):
    return pl.BlockSpec(
        (None, None, 1, 3 * D),
        lambda i: (layer, _cond_row(i, tm, tokens_per_cond, first_row), 0, which))


def _const_spec(shape, index):
    return pl.BlockSpec(shape, lambda i: index, pipeline_mode=pl.Buffered(1))


def _ffn_kernel(*refs, pre_outproj, side=None):
    if pre_outproj:
        (x_ref, mix_ref, wout_ref, modp_ref, mod_ref, g_ref, wi_ref, wo_ref,
         o_ref, a_ref, h_ref) = refs
        x1_ref = o_ref
    else:
        x_ref, mod_ref, g_ref, wi_ref, wo_ref, o_ref, a_ref, h_ref = refs
        x1_ref = x_ref
    n_sub = x_ref.shape[0] // FFN_SUB

    def rows(s):
        return slice(s * FFN_SUB, (s + 1) * FFN_SUB)

    def outproj(s):
        if pre_outproj:
            x1_ref[rows(s), :] = x_ref[rows(s), :] + modp_ref[:, 2 * D:3 * D] * _dot(
                mix_ref[rows(s), :], wout_ref[...])

    def prologue(s, piece):
        r0 = s * FFN_SUB + piece * FFN_PIECE
        p0 = piece * FFN_PIECE
        x = x1_ref[r0:r0 + FFN_PIECE, :]
        h = _rms_mod(x, g_ref[...], mod_ref[:, D:2 * D], mod_ref[:, 0:D]).astype(BF16)
        h_ref[s, p0:p0 + FFN_PIECE, :] = h
        token = h[0:16, :]
        for r in range(16, FFN_PIECE, 16):
            token = token + h[r:r + 16, :]
        return functools.reduce(lambda a, b: a + b, [token[:, l:l + 128] for l in range(0, D, 128)])

    def anchor(s, token):
        zero = jnp.zeros_like(token)
        h_ref[s, 0:16, 0:128] = h_ref[s, 0:16, 0:128] + jnp.maximum(jnp.minimum(token, zero), zero)

    n_piece = FFN_SUB // FFN_PIECE
    outproj(0)
    for piece in range(n_piece):
        prologue(0, piece)
    n_chunk = -(-D_FF // FFN_CHUNK)
    for s in range(n_sub):
        items = [] if side is None else side[s]
        per_chunk = -(-len(items) // n_chunk)
        if s + 1 < n_sub:
            outproj(s + 1)
        for c, c0 in enumerate(range(0, D_FF, FFN_CHUNK)):
            c1 = min(c0 + FFN_CHUNK, D_FF)
            gate = _dot(h_ref[s], wi_ref[:, c0:c1])
            up = _dot(h_ref[s], wi_ref[:, D_FF + c0:D_FF + c1])
            a_ref[s, :, c0:c1] = (gate * jax.nn.sigmoid(gate) * up).astype(BF16)
            if s + 1 < n_sub and c < n_piece:
                anchor(s, prologue(s + 1, c))
            for item in items[c * per_chunk:(c + 1) * per_chunk]:
                item()
        o_ref[rows(s), :] = x1_ref[rows(s), :] + (0.5 * mod_ref[:, 2 * D:3 * D]) * _dot(
            a_ref[s], wo_ref[...])


def _with_casts(body, n_in, n_out, n_cast):
    if n_cast == 0:
        return body

    def kernel(*refs):
        ins, srcs = refs[:n_in], refs[n_in:n_in + n_cast]
        outs = refs[n_in + n_cast:n_in + n_cast + n_out]
        dsts = refs[n_in + n_cast + n_out:n_in + 2 * n_cast + n_out]
        for src, dst in zip(srcs, dsts):
            dst[...] = src[...].astype(BF16)
        body(*ins, *outs, *refs[n_in + 2 * n_cast + n_out:])

    return kernel


def _cast_specs(casts, steps):
    in_specs, out_specs, out_shapes, args = [], [], [], []
    for w, layer in casts:
        _, r, c = w.shape
        rows = r // steps
        assert r % steps == 0 and rows % 16 == 0
        in_specs.append(pl.BlockSpec((None, rows, c), lambda i, layer=layer: (layer, i, 0)))
        out_specs.append(pl.BlockSpec((rows, c), lambda i: (i, 0)))
        out_shapes.append(jax.ShapeDtypeStruct((r, c), BF16))
        args.append(w)
    return in_specs, out_specs, out_shapes, args


def _ffn(x, mods, norm_g, wi, wo, layer, which, tokens_per_cond, first_row,
         mix=None, w_out=None, casts=()):
    t = x.shape[0]
    tm = TM_FFN
    tok = pl.BlockSpec((tm, D), lambda i: (i, 0))
    in_specs = [tok]
    args = [x]
    if mix is not None:
        in_specs += [tok, _const_spec((MIX_WIDTH, D), (0, 0)),
                     _mod_spec(layer, 1, tm, tokens_per_cond, first_row)]
        args += [mix, w_out, mods]
    in_specs += [
        _mod_spec(layer, which, tm, tokens_per_cond, first_row),
        _const_spec((None, None, 1, D), (layer, which, 0, 0)),
        _const_spec((D, 2 * D_FF), (0, 0)),
        _const_spec((D_FF, D), (0, 0)),
    ]
    args += [mods, norm_g, wi, wo]
    c_in, c_out, c_shapes, c_args = _cast_specs(casts, t // tm)
    return pl.pallas_call(
        _with_casts(functools.partial(_ffn_kernel, pre_outproj=mix is not None),
                    len(in_specs), 1, len(casts)),
        grid=(t // tm,),
        in_specs=in_specs + c_in,
        out_specs=[tok] + c_out,
        out_shape=[jax.ShapeDtypeStruct((t, D), F32)] + c_shapes,
        scratch_shapes=[pltpu.VMEM((tm // FFN_SUB, FFN_SUB, D_FF), BF16),
                        pltpu.VMEM((tm // FFN_SUB, FFN_SUB, D), BF16)],
        compiler_params=pltpu.CompilerParams(
            dimension_semantics=("arbitrary",), vmem_limit_bytes=VMEM_LIMIT),
        name="ffn_outproj" if mix is not None else "ffn",
    )(*args, *c_args)


def _head_norm(x, m, g):
    ms = _dot((x * x).astype(BF16), m)
    return x * lax.rsqrt(ms + EPS) * g


def _rope(x, c, s_up, s_dn):
    n = x.shape[1]
    return x * c + pltpu.roll(x, n - 16, 1) * s_up + pltpu.roll(x, 16, 1) * s_dn


def _proj_kernel(*refs, rope):
    if rope:
        (x_ref, mod_ref, g_ref, win_ref, mq_ref, mk_ref, qg_ref, kg_ref,
         c_ref, su_ref, sd_ref, pool_ref, glu_ref, q_ref, k_ref, v_ref) = refs
    else:
        (x_ref, mod_ref, g_ref, win_ref, mq_ref, mk_ref, qg_ref, kg_ref,
         pool_ref, glu_ref, q_ref, k_ref, v_ref) = refs
    for r0 in range(0, x_ref.shape[0], PROJ_SUB):
        rows = slice(r0, r0 + PROJ_SUB)
        h = _rms_mod(x_ref[rows, :], g_ref[...], mod_ref[:, D:2 * D], mod_ref[:, 0:D]).astype(BF16)
        u = _dot(h, win_ref[...])
        pool_ref[rows, :] = u[:, :POOL_WIDTH]
        glu_ref[rows, :] = (u[:, POOL_WIDTH:POOL_WIDTH + CONV_WIDTH]
                            * jax.nn.sigmoid(u[:, POOL_WIDTH + CONV_WIDTH:ATTN_OFFSET]))
        q = _head_norm(u[:, ATTN_OFFSET:ATTN_OFFSET + ATTN_WIDTH], mq_ref[...], qg_ref[...])
        k = _head_norm(u[:, ATTN_OFFSET + ATTN_WIDTH:ATTN_OFFSET + ATTN_WIDTH + KV_WIDTH],
                       mk_ref[...], kg_ref[...])
        if rope:
            c, su, sd = c_ref[rows, :], su_ref[rows, :], sd_ref[rows, :]
            k = _rope(k, c, su, sd)
            rep = ATTN_WIDTH // KV_WIDTH
            q = _rope(q, jnp.concatenate([c] * rep, axis=1), jnp.concatenate([su] * rep, axis=1),
                      jnp.concatenate([sd] * rep, axis=1))
        q_ref[rows, :] = q.astype(BF16)
        k_ref[rows, :] = k
        v_ref[rows, :] = u[:, ATTN_OFFSET + ATTN_WIDTH + KV_WIDTH:]


def _proj(x, mods, norm_g, w_in, mq, mk, qg, kg, layer, tokens_per_cond, first_row,
          rope_tables=None, casts=()):
    t = x.shape[0]
    tm = TM_PROJ
    in_specs = [
        pl.BlockSpec((tm, D), lambda i: (i, 0)),
        _mod_spec(layer, 1, tm, tokens_per_cond, first_row),
        _const_spec((None, None, 1, D), (layer, 1, 0, 0)),
        _const_spec((D, IN_WIDTH), (0, 0)),
        _const_spec((ATTN_WIDTH, ATTN_WIDTH), (0, 0)),
        _const_spec((KV_WIDTH, KV_WIDTH), (0, 0)),
        _const_spec((None, 1, ATTN_WIDTH), (layer, 0, 0)),
        _const_spec((None, 1, KV_WIDTH), (layer, 0, 0)),
    ]
    args = [x, mods, norm_g, w_in, mq, mk, qg, kg]
    if rope_tables is not None:
        seq_tiles = rope_tables[0].shape[0] // tm
        in_specs += [pl.BlockSpec((tm, KV_WIDTH), lambda i: (i % seq_tiles, 0))] * 3
        args += list(rope_tables)

    def out(width):
        return pl.BlockSpec((tm, width), lambda i: (i, 0))

    c_in, c_out, c_shapes, c_args = _cast_specs(casts, t // tm)
    out_specs = [out(POOL_WIDTH), out(CONV_WIDTH), out(ATTN_WIDTH), out(KV_WIDTH), out(KV_WIDTH)]
    return pl.pallas_call(
        _with_casts(functools.partial(_proj_kernel, rope=rope_tables is not None),
                    len(in_specs), len(out_specs), len(casts)),
        grid=(t // tm,),
        in_specs=in_specs + c_in,
        out_specs=out_specs + c_out,
        out_shape=[
            jax.ShapeDtypeStruct((t, POOL_WIDTH), F32),
            jax.ShapeDtypeStruct((t, CONV_WIDTH), F32),
            jax.ShapeDtypeStruct((t, ATTN_WIDTH), BF16),
            jax.ShapeDtypeStruct((t, KV_WIDTH), F32),
            jax.ShapeDtypeStruct((t, KV_WIDTH), F32),
        ] + c_shapes,
        compiler_params=pltpu.CompilerParams(
            dimension_semantics=("arbitrary",), vmem_limit_bytes=VMEM_LIMIT),
        name="proj_rope" if rope_tables is not None else "proj",
    )(*args, *c_args)


def _fill_ext(ext_ref, prev_ref, cur_ref, next_ref, i, nb):
    ext_ref[0:HALO, :] = jnp.where(i > 0, prev_ref[...], 0.0)
    ext_ref[HALO:HALO + TB, :] = cur_ref[...]
    ext_ref[HALO + TB:, :] = jnp.where(i < nb - 1, next_ref[...], 0.0)


def _shift_copies(ext_ref, z_ref, shifts, lanes):
    for b in shifts:
        z_ref[b, :, lanes] = ext_ref[b:b + ZROWS, lanes]


def _window(z_ref, ext_off, r0, lanes):
    a, b = divmod(ext_off, 8)
    return z_ref[b, 8 * a + r0:8 * a + r0 + ROW_CHUNK, lanes]


def _pool_chunk(z_ref, pooled_ref, r0, i, n):
    lo_lanes, hi_lanes = slice(0, 128), slice(128, 256)
    first = lax.broadcasted_iota(jnp.int32, (ROW_CHUNK, 128), 1) < POOL_GROUP_DIM
    t = i * TB + r0 + lax.broadcasted_iota(jnp.int32, (ROW_CHUNK, 128), 0)

    def centred(total, half, cur):
        count = jnp.minimum(t + half, n) - jnp.maximum(t - half, 0)
        return (total / count.astype(F32) - cur).astype(BF16)

    def u(off, lanes):
        return _window(z_ref, HALO + off, r0, lanes)

    cur = u(0, lo_lanes)
    acc2 = u(-1, lo_lanes) + cur
    acc4 = acc2 + u(-2, lo_lanes) + u(1, lo_lanes)
    pooled_ref[r0:r0 + ROW_CHUNK, lo_lanes] = centred(
        jnp.where(first, acc2, acc4), jnp.where(first, 1, 2), cur)
    cur = u(0, hi_lanes)
    acc8 = cur
    for off in (-4, -3, -2, -1, 1, 2, 3):
        acc8 = acc8 + u(off, hi_lanes)
    acc16 = acc8
    for off in (-8, -7, -6, -5, 4, 5, 6, 7):
        acc16 = acc16 + u(off, hi_lanes)
    pooled_ref[r0:r0 + ROW_CHUNK, hi_lanes] = centred(
        jnp.where(first, acc8, acc16), jnp.where(first, 4, 8), cur)


def _conv_chunk(z_ref, act_ref, r0, dw_ref, cb_ref, cg_ref):
    acc = None
    for k in range(CONV_TAPS):
        term = _window(z_ref, HALO - CONV_TAPS // 2 + k, r0, slice(None)) * dw_ref[k:k + 1, :]
        acc = term if acc is None else acc + term
    y = acc + cb_ref[...]
    ms = jnp.mean(y * y, axis=-1, keepdims=True)
    z = y * lax.rsqrt(ms + EPS) * cg_ref[...]
    act_ref[r0:r0 + ROW_CHUNK, :] = (z * jax.nn.sigmoid(z)).astype(BF16)


def _att_scores(s_ref, q_ref, q0, j, kb, biases):
    lane = lax.broadcasted_iota(jnp.int32, (BLOCK, KV_WIDTH), 1)
    own = (lane // HEAD_DIM) == j
    qs = []
    for h in range(j * Q_GROUP, (j + 1) * Q_GROUP):
        c0 = (h // 2) * KV_WIDTH
        qh = q_ref[q0:q0 + BLOCK, c0:c0 + KV_WIDTH]
        if h % 2 != j:
            qh = jnp.concatenate([qh[:, HEAD_DIM:], qh[:, :HEAD_DIM]], axis=1)
        qs.append(jnp.where(own, qh, jnp.zeros_like(qh)))
    qj = jnp.concatenate(qs, axis=0)
    s = lax.dot_general(kb, qj, (((1,), (1,)), ((), ())), preferred_element_type=F32)
    if biases is None:
        s_ref[...] = s
    else:
        s_ref[0:BLOCK] = s[0:BLOCK] + biases[0]
        s_ref[BLOCK:2 * BLOCK] = s[BLOCK:2 * BLOCK]
        s_ref[2 * BLOCK:3 * BLOCK] = s[2 * BLOCK:3 * BLOCK] + biases[1]
        s_ref[3 * BLOCK:] = s[3 * BLOCK:]


def _att_weights(p_ref, s_ref, j, sink_ref, layer):
    sink = jnp.concatenate(
        [jnp.full((1, BLOCK), sink_ref[layer * N_HEADS + j * Q_GROUP + g] * LOG2E, F32)
         for g in range(Q_GROUP)], axis=1)
    m = jnp.maximum(jnp.max(s_ref[...], axis=0, keepdims=True), sink)
    for r in range(0, s_ref.shape[0], BLOCK):
        p_ref[r:r + BLOCK] = jnp.exp2(s_ref[r:r + BLOCK] - m).astype(BF16)
    return jnp.exp2(sink - m)


def _att_values(p_ref, sink_w, j, vt):
    v_ones = jnp.concatenate(
        [vt[j * HEAD_DIM:(j + 1) * HEAD_DIM, :], jnp.ones((16, vt.shape[1]), BF16)], axis=0)
    o = _dot(v_ones, p_ref[...])
    return o[0:HEAD_DIM, :] / (o[HEAD_DIM:HEAD_DIM + 1, :] + sink_w)


N_ATT_STAGES = (TB // BLOCK) * N_KV_HEADS + 2


def _mixer_parts(refs, layer, latent, n, i):
    nb = n // TB
    (pp_ref, pc_ref, pn_ref, gp_ref, gc_ref, gn_ref, q_ref) = refs[:7]
    rest = refs[7:]
    if latent:
        (kp_ref, kc_ref, kn_ref, vp_ref, vc_ref, vn_ref, ck_ref, cv_ref) = rest[:8]
        rest = rest[8:]
    else:
        (ka_ref, va_ref) = rest[:2]
        rest = rest[2:]
    (poolw_ref, pscale_ref, dw_ref, cb_ref, cg_ref, pw_ref, sink_ref,
     o_ref, pext_ref, gext_ref, zp_ref, zg_ref, pooled_ref, act_ref,
     s_ref, p_ref, at_ref) = rest

    operands = []

    def prep():
        _fill_ext(pext_ref, pp_ref, pc_ref, pn_ref, i, nb)
        _shift_copies(pext_ref, zp_ref, (6, 7, 0, 1), slice(0, 128))
        _shift_copies(pext_ref, zp_ref, range(8), slice(128, 256))
        _fill_ext(gext_ref, gp_ref, gc_ref, gn_ref, i, nb)
        _shift_copies(gext_ref, zg_ref, range(8), slice(None))
        if latent:
            k_seq = jnp.concatenate([kp_ref[...], kc_ref[...], kn_ref[...]], axis=0).astype(BF16)
            vt_seq = jnp.concatenate(
                [vp_ref[...], vc_ref[...], vn_ref[...]], axis=0).T.astype(BF16)
            k_ctx, vt_ctx = ck_ref[...].astype(BF16), cv_ref[...].T.astype(BF16)
            cols = Q_GROUP * BLOCK
            key = lax.broadcasted_iota(jnp.int32, (BLOCK, cols), 0)
            r = lax.broadcasted_iota(jnp.int32, (BLOCK, cols), 1) % BLOCK
        else:
            k_all, vt_all = ka_ref[...].astype(BF16), va_ref[...].T.astype(BF16)
        for qb in range(TB // BLOCK):
            q0 = qb * BLOCK
            biases = None
            if latent:
                k_all = jnp.concatenate([k_seq[q0:q0 + 3 * BLOCK], k_ctx], axis=0)
                vt_all = jnp.concatenate([vt_seq[:, q0:q0 + 3 * BLOCK], vt_ctx], axis=1)
                blk = i * (TB // BLOCK) + qb
                biases = (jnp.where((key >= r) & (blk > 0), 0.0, NEG_INF).astype(F32),
                          jnp.where((key <= r) & (blk < n // BLOCK - 1), 0.0, NEG_INF).astype(F32))
            operands.append((q0, k_all, vt_all, biases))

    vpu_chunks = []
    for r0 in range(0, TB, ROW_CHUNK):
        vpu_chunks.append(functools.partial(_pool_chunk, zp_ref, pooled_ref, r0, i, n))
        vpu_chunks.append(functools.partial(_conv_chunk, zg_ref, act_ref, r0, dw_ref, cb_ref, cg_ref))

    def pool_conv_matmuls():
        o_ref[:, 0:POOL_WIDTH] = (
            _dot(pooled_ref[...], poolw_ref[...]) * pscale_ref[...]).astype(BF16)
        o_ref[:, POOL_WIDTH:POOL_WIDTH + CONV_WIDTH] = _dot(act_ref[...], pw_ref[...]).astype(BF16)

    units = [(qb, j) for qb in range(TB // BLOCK) for j in range(N_KV_HEADS)]
    sink_w = {}

    def att_stage(t):
        if t < len(units):
            qb, j = units[t]
            q0, k_all, _, biases = operands[qb]
            _att_scores(s_ref.at[t % 2], q_ref, q0, j, k_all, biases)
        if 0 <= t - 1 < len(units):
            u = t - 1
            sink_w[u] = _att_weights(p_ref.at[u % 2], s_ref.at[u % 2], units[u][1], sink_ref, layer)
        if 0 <= t - 2 < len(units):
            u = t - 2
            qb, j = units[u]
            o = _att_values(p_ref.at[u % 2], sink_w.pop(u), j, operands[qb][2])
            for g in range(Q_GROUP):
                c0 = (j * Q_GROUP + g) * HEAD_DIM
                at_ref[qb, c0:c0 + HEAD_DIM, :] = o[:, g * BLOCK:(g + 1) * BLOCK]

    def att_store():
        for qb in range(TB // BLOCK):
            o_ref[qb * BLOCK:(qb + 1) * BLOCK, POOL_WIDTH + CONV_WIDTH:] = (
                at_ref[qb].T.astype(BF16))

    return prep, att_stage, vpu_chunks, pool_conv_matmuls, att_store


def _mixer_items(parts):
    prep, att_stage, vpu_chunks, pool_conv_matmuls, att_store = parts
    per_stage = -(-len(vpu_chunks) // N_ATT_STAGES)

    def stage(t):
        att_stage(t)
        for chunk in vpu_chunks[t * per_stage:(t + 1) * per_stage]:
            chunk()

    def finish():
        pool_conv_matmuls()
        att_store()

    return [prep] + [functools.partial(stage, t) for t in range(N_ATT_STAGES)] + [finish]


def _mixer_kernel(*refs, layer, latent, n):
    prep, att_stage, vpu_chunks, pool_conv_matmuls, att_store = _mixer_parts(
        refs, layer, latent, n, pl.program_id(1))
    prep()
    if latent:
        for chunk in vpu_chunks:
            chunk()
        pool_conv_matmuls()
        for t in range(N_ATT_STAGES):
            att_stage(t)
    else:
        per_stage = -(-len(vpu_chunks) // N_ATT_STAGES)
        for t in range(N_ATT_STAGES):
            att_stage(t)
            for chunk in vpu_chunks[t * per_stage:(t + 1) * per_stage]:
                chunk()
        pool_conv_matmuls()
    att_store()


def _mixer(pool_in, glu, q, k, v, wts, layer, cache=None):
    b, n, _ = pool_in.shape
    nb = n // TB
    hpb = TB // HALO
    nh = n // HALO
    kpb = TB // BLOCK
    nk = n // BLOCK
    n_keys = n if cache is None else 3 * BLOCK + cache[0].shape[2]

    def cur(width):
        return pl.BlockSpec((None, TB, width), lambda bi, i: (bi, i, 0))

    def halo_prev(width):
        return pl.BlockSpec((None, HALO, width), lambda bi, i: (bi, jnp.maximum(i * hpb - 1, 0), 0))

    def halo_next(width):
        return pl.BlockSpec((None, HALO, width),
                            lambda bi, i: (bi, jnp.minimum((i + 1) * hpb, nh - 1), 0))

    in_specs = [halo_prev(POOL_WIDTH), cur(POOL_WIDTH), halo_next(POOL_WIDTH),
                halo_prev(CONV_WIDTH), cur(CONV_WIDTH), halo_next(CONV_WIDTH),
                cur(ATTN_WIDTH)]
    args = [pool_in, pool_in, pool_in, glu, glu, glu, q]
    if cache is not None:
        cache_k, cache_v = cache
        past = cache_k.shape[2]
        blk_prev = pl.BlockSpec((None, BLOCK, KV_WIDTH),
                                lambda bi, i: (bi, jnp.maximum(i * kpb - 1, 0), 0))
        blk_next = pl.BlockSpec((None, BLOCK, KV_WIDTH),
                                lambda bi, i: (bi, jnp.minimum((i + 1) * kpb, nk - 1), 0))
        cspec = pl.BlockSpec((None, None, past, KV_WIDTH), lambda bi, i: (bi, layer, 0, 0))
        in_specs += [blk_prev, cur(KV_WIDTH), blk_next, blk_prev, cur(KV_WIDTH), blk_next,
                     cspec, cspec]
        args += [k, k, k, v, v, v, cache_k, cache_v]
    else:
        whole = pl.BlockSpec((None, n, KV_WIDTH), lambda bi, i: (bi, 0, 0))
        in_specs += [whole, whole]
        args += [k, v]
    in_specs += _mixer_weight_specs(layer)
    args += _mixer_weight_args(wts)
    return pl.pallas_call(
        functools.partial(_mixer_kernel, layer=layer, latent=cache is not None, n=n),
        grid=(b, nb),
        in_specs=in_specs,
        out_specs=pl.BlockSpec((None, TB, MIX_WIDTH), lambda bi, i: (bi, i, 0)),
        out_shape=jax.ShapeDtypeStruct((b, n, MIX_WIDTH), BF16),
        scratch_shapes=_mixer_scratch(n_keys),
        compiler_params=pltpu.CompilerParams(
            dimension_semantics=("arbitrary", "arbitrary"), vmem_limit_bytes=VMEM_LIMIT),
        name="mixer_latent" if cache is not None else "mixer_context",
    )(*args)


def _mixer_weight_specs(layer):
    def const(shape, index):
        return pl.BlockSpec(shape, lambda *_: index)

    return [
        const((None, POOL_WIDTH, POOL_WIDTH), (layer, 0, 0)),
        const((None, 1, POOL_WIDTH), (layer, 0, 0)),
        const((None, CONV_TAPS, CONV_WIDTH), (layer, 0, 0)),
        const((None, 1, CONV_WIDTH), (layer, 0, 0)),
        const((None, 1, CONV_WIDTH), (layer, 0, 0)),
        const((None, CONV_WIDTH, CONV_WIDTH), (layer, 0, 0)),
        pl.BlockSpec(memory_space=pltpu.SMEM),
    ]


def _mixer_weight_args(wts):
    return [wts["pool_bd"], wts["pool_scale"], wts["conv_dw"], wts["conv_b"],
            wts["conv_norm_g"], wts["conv_pw"], wts["sink"]]


def _mixer_scratch(n_keys):
    return [pltpu.VMEM((TB + 2 * HALO, POOL_WIDTH), F32),
            pltpu.VMEM((TB + 2 * HALO, CONV_WIDTH), F32),
            pltpu.VMEM((8, ZROWS, POOL_WIDTH), F32),
            pltpu.VMEM((8, ZROWS, CONV_WIDTH), F32),
            pltpu.VMEM((TB, POOL_WIDTH), BF16),
            pltpu.VMEM((TB, CONV_WIDTH), BF16),
            pltpu.VMEM((2, n_keys, Q_GROUP * BLOCK), F32),
            pltpu.VMEM((2, n_keys, Q_GROUP * BLOCK), BF16),
            pltpu.VMEM((TB // BLOCK, ATTN_WIDTH, BLOCK), F32)]


N_CTX_BLOCK_OPERANDS = 9


def _ffn_side_kernel(*refs, n_ffn_in, n_mix_in, n_ffn_scratch, layer, n):
    ffn_in = refs[:n_ffn_in]
    mix_in = refs[n_ffn_in:n_ffn_in + n_mix_in]
    o_ref, mix_out = refs[n_ffn_in + n_mix_in:n_ffn_in + n_mix_in + 2]
    scratch = refs[n_ffn_in + n_mix_in + 2:]
    ffn_scratch, mix_scratch = scratch[:n_ffn_scratch], scratch[n_ffn_scratch:]
    side = []
    for s in range(mix_out.shape[0]):
        block = (tuple(r.at[s] for r in mix_in[:N_CTX_BLOCK_OPERANDS])
                 + tuple(mix_in[N_CTX_BLOCK_OPERANDS:]) + (mix_out.at[s],) + tuple(mix_scratch))
        side.append(_mixer_items(_mixer_parts(block, layer, False, n, 0)))
    _ffn_kernel(*ffn_in, o_ref, *ffn_scratch, pre_outproj=False, side=side)


def _ffn_with_context_mixer(x, mods, norm_g, wi, wo, layer, which, tokens_per_cond, first_row,
                            pool_in, glu, q, k, v, wts, casts=()):
    t = x.shape[0]
    tm = TM_FFN
    b, n, _ = pool_in.shape
    per_step = tm // FFN_SUB
    assert n == TB and b == (t // tm) * per_step
    tok = pl.BlockSpec((tm, D), lambda i: (i, 0))
    ffn_specs = [
        tok,
        _mod_spec(layer, which, tm, tokens_per_cond, first_row),
        _const_spec((None, None, 1, D), (layer, which, 0, 0)),
        _const_spec((D, 2 * D_FF), (0, 0)),
        _const_spec((D_FF, D), (0, 0)),
    ]
    ffn_args = [x, mods, norm_g, wi, wo]

    def seqs(rows, width):
        return pl.BlockSpec((per_step, rows, width), lambda i: (i, 0, 0))

    mix_specs = [seqs(HALO, POOL_WIDTH), seqs(TB, POOL_WIDTH), seqs(HALO, POOL_WIDTH),
                 seqs(HALO, CONV_WIDTH), seqs(TB, CONV_WIDTH), seqs(HALO, CONV_WIDTH),
                 seqs(TB, ATTN_WIDTH), seqs(TB, KV_WIDTH), seqs(TB, KV_WIDTH)]
    mix_args = [pool_in, pool_in, pool_in, glu, glu, glu, q, k, v]
    assert len(mix_specs) == N_CTX_BLOCK_OPERANDS
    mix_specs += _mixer_weight_specs(layer)
    mix_args += _mixer_weight_args(wts)
    ffn_scratch = [pltpu.VMEM((tm // FFN_SUB, FFN_SUB, D_FF), BF16),
                   pltpu.VMEM((tm // FFN_SUB, FFN_SUB, D), BF16)]
    c_in, c_out, c_shapes, c_args = _cast_specs(casts, t // tm)
    body = functools.partial(_ffn_side_kernel, n_ffn_in=len(ffn_specs), n_mix_in=len(mix_specs),
                             n_ffn_scratch=len(ffn_scratch), layer=layer, n=n)
    return pl.pallas_call(
        _with_casts(body, len(ffn_specs) + len(mix_specs), 2, len(casts)),
        grid=(t // tm,),
        in_specs=ffn_specs + mix_specs + c_in,
        out_specs=[tok, pl.BlockSpec((per_step, TB, MIX_WIDTH), lambda i: (i, 0, 0))] + c_out,
        out_shape=[jax.ShapeDtypeStruct((t, D), F32),
                   jax.ShapeDtypeStruct((b, n, MIX_WIDTH), BF16)] + c_shapes,
        scratch_shapes=ffn_scratch + _mixer_scratch(n),
        compiler_params=pltpu.CompilerParams(
            dimension_semantics=("arbitrary",), vmem_limit_bytes=VMEM_LIMIT),
        name="ffn_with_context_mixer",
    )(*ffn_args, *mix_args, *c_args)


def _rope_tables(n):
    t = np.arange(n)
    half = HEAD_DIM // 2
    inv = ROPE_THETA ** (-np.arange(0, half, 2, dtype=np.float64) / half)
    zeros = np.zeros((n, half // 2))
    cos, s_up, s_dn = [], [], []
    for pos in (t // GRID_W, t % GRID_W):
        ang = pos[:, None].astype(np.float64) * inv[None, :]
        cos += [np.cos(ang), np.cos(ang)]
        s_up += [-np.sin(ang), zeros]
        s_dn += [zeros, np.sin(ang)]
    return tuple(jnp.asarray(np.concatenate(parts * N_KV_HEADS, axis=1), dtype=F32)
                 for parts in (cos, s_up, s_dn))


def _block_diag_mean(width):
    idx = jnp.arange(width) // HEAD_DIM
    return jnp.where(idx[:, None] == idx[None, :], 1.0 / HEAD_DIM, 0.0).astype(BF16)


def kernel(x_prompt, x_sample, cache_k, cache_v, c, c_ctx, mod_w, mod_b, norm_g, ffn1_wi, ffn1_wo, ffn2_wi, ffn2_wo, w_in, w_out, pool_w, pool_scale, conv_dw, conv_b, conv_norm_g, conv_pw, q_norm_g, k_norm_g, sink):
    batch, seq, _ = x_prompt.shape
    dec_batch, dec_seq, _ = x_sample.shape
    past = cache_k.shape[2]
    assert 1 + dec_batch <= COND_ROWS

    cond = jnp.zeros((COND_ROWS, D), F32).at[0].set(c_ctx).at[1:1 + dec_batch].set(c)
    mods = _modulations(cond, mod_w, mod_b).reshape(DEPTH, COND_ROWS, 1, N_MOD * D)

    norm_g4 = norm_g.reshape(DEPTH, 3, 1, D)
    wi1, wo1 = ffn1_wi[0].astype(BF16), ffn1_wo[0].astype(BF16)
    wi2 = wo2 = w_in_b = w_out_b = None
    eye = jnp.eye(POOL_GROUPS, dtype=F32)
    pool_bd = (eye[None, :, None, :, None] * pool_w[:, :, :, None, :]).reshape(
        DEPTH, POOL_WIDTH, POOL_WIDTH).astype(BF16)
    wts = {
        "pool_bd": pool_bd,
        "pool_scale": pool_scale.reshape(DEPTH, 1, POOL_WIDTH),
        "conv_dw": conv_dw,
        "conv_b": conv_b.reshape(DEPTH, 1, CONV_WIDTH),
        "conv_norm_g": conv_norm_g.reshape(DEPTH, 1, CONV_WIDTH),
        "conv_pw": conv_pw.astype(BF16),
        "sink": sink.reshape(DEPTH * N_HEADS),
    }
    mq, mk = _block_diag_mean(ATTN_WIDTH), _block_diag_mean(KV_WIDTH)
    qg = jnp.tile(q_norm_g, (1, N_HEADS)).reshape(DEPTH, 1, ATTN_WIDTH) * (SCALE * LOG2E)
    kg = jnp.tile(k_norm_g, (1, N_KV_HEADS)).reshape(DEPTH, 1, KV_WIDTH)
    rope = _rope_tables(dec_seq)
    cache_k4 = cache_k.reshape(dec_batch, DEPTH, past, KV_WIDTH)
    cache_v4 = cache_v.reshape(dec_batch, DEPTH, past, KV_WIDTH)

    def seqs(a, bsz, n):
        return a.reshape(bsz, n, a.shape[-1])

    yp = x_prompt.reshape(batch * seq, D)
    ys = x_sample.reshape(dec_batch * dec_seq, D)
    ks, vs = [], []
    for layer in range(DEPTH):
        def nxt(*ws):
            return [(w, layer + 1) for w in ws] if layer + 1 < DEPTH else []

        def first(*ws):
            return [(w, 0) for w in ws] if layer == 0 else []

        yp, *now = _ffn(yp, mods, norm_g4, wi1, wo1, layer, 0, None, 0,
                        casts=first(w_in, w_out, ffn2_wo))
        if layer == 0:
            w_in_b, w_out_b, wo2 = now
        *c_parts, = _proj(yp, mods, norm_g4, w_in_b, mq, mk, qg, kg, layer, None, 0,
                          casts=first(ffn2_wi))
        if layer == 0:
            wi2 = c_parts.pop()
        ys, mix_c, *wo1_n = _ffn_with_context_mixer(
            ys, mods, norm_g4, wi1, wo1, layer, 0, dec_seq, 1,
            *(seqs(a, batch, seq) for a in c_parts), wts, casts=nxt(ffn1_wo))
        l_parts = _proj(ys, mods, norm_g4, w_in_b, mq, mk, qg, kg, layer, dec_seq, 1,
                        rope_tables=rope)
        yp, *wi2_n = _ffn(yp, mods, norm_g4, wi2, wo2, layer, 2, None, 0,
                          mix=mix_c.reshape(batch * seq, MIX_WIDTH), w_out=w_out_b,
                          casts=nxt(ffn2_wi))
        mix_l = _mixer(*(seqs(a, dec_batch, dec_seq) for a in l_parts), wts, layer,
                       cache=(cache_k4, cache_v4))
        ys, *rest_n = _ffn(ys, mods, norm_g4, wi2, wo2, layer, 2, dec_seq, 1,
                           mix=mix_l.reshape(dec_batch * dec_seq, MIX_WIDTH), w_out=w_out_b,
                           casts=nxt(ffn2_wo, w_in, w_out, ffn1_wi))
        ks.append(c_parts[3].reshape(batch, seq, N_KV_HEADS, HEAD_DIM))
        vs.append(c_parts[4].reshape(batch, seq, N_KV_HEADS, HEAD_DIM))
        if layer + 1 < DEPTH:
            (wo1,), (wi2,), (wo2, w_in_b, w_out_b, wi1) = wo1_n, wi2_n, rest_n
    return (yp.reshape(batch, seq, D), ys.reshape(dec_batch, dec_seq, D),
            jnp.stack(ks, axis=1), jnp.stack(vs, axis=1))
```

```python
import functools

import jax
import jax.numpy as jnp
import numpy as np
from jax import lax
from jax.experimental import pallas as pl
from jax.experimental.pallas import tpu as pltpu

D = 1024
DEPTH = 2
GRID_W = 64
POOL_WIDTH = 256
POOL_GROUPS = 4
POOL_GROUP_DIM = POOL_WIDTH // POOL_GROUPS
CONV_WIDTH = 256
CONV_TAPS = 31
N_HEADS = 8
N_KV_HEADS = 2
HEAD_DIM = 64
Q_GROUP = N_HEADS // N_KV_HEADS
ATTN_WIDTH = N_HEADS * HEAD_DIM
KV_WIDTH = N_KV_HEADS * HEAD_DIM
MIX_WIDTH = POOL_WIDTH + CONV_WIDTH + ATTN_WIDTH
ATTN_OFFSET = POOL_WIDTH + 2 * CONV_WIDTH
IN_WIDTH = ATTN_OFFSET + ATTN_WIDTH + 2 * KV_WIDTH
BLOCK = 128
D_FF = 2816
N_MOD = 9
ROPE_THETA = 10000.0
EPS = 1e-6
NEG_INF = -1e30
SCALE = HEAD_DIM ** -0.5
LOG2E = 1.4426950408889634

COND_ROWS = 8
HALO = 16
TM_FFN = 1024
FFN_SUB = 512
FFN_PIECE = 128
TM_PROJ = 1024
TM_PROJ_ROPE = 2048
PROJ_SUB = 256
MXU_TILE = 256
FFN_CHUNK = 2 * MXU_TILE
TB = 256
ROW_CHUNK = 64
ZROWS = TB + 24
VMEM_LIMIT = 56 * 1024 * 1024

F32 = jnp.float32
BF16 = jnp.bfloat16


def _dot(a, b):
    return jnp.dot(a, b, preferred_element_type=F32)


def _rms_mod(x, g, sc, sh):
    ms = jnp.mean(x * x, axis=-1, keepdims=True)
    return (x * lax.rsqrt(ms + EPS) * g) * (1.0 + sc) + sh


def _mod_kernel(cond_ref, w_ref, b_ref, o_ref):
    c = cond_ref[...]
    s = (c * jax.nn.sigmoid(c)).astype(BF16)
    o_ref[...] = _dot(s, w_ref[...].astype(BF16)) + b_ref[...]


def _modulations(cond, mod_w, mod_b):
    tn = 1024
    nt = (N_MOD * D) // tn
    return pl.pallas_call(
        _mod_kernel,
        grid=(DEPTH, nt),
        in_specs=[
            pl.BlockSpec((COND_ROWS, D), lambda l, j: (0, 0)),
            pl.BlockSpec((None, D, tn), lambda l, j: (l, 0, j)),
            pl.BlockSpec((None, 1, tn), lambda l, j: (l, 0, j)),
        ],
        out_specs=pl.BlockSpec((None, COND_ROWS, tn), lambda l, j: (l, 0, j)),
        out_shape=jax.ShapeDtypeStruct((DEPTH, COND_ROWS, N_MOD * D), F32),
        compiler_params=pltpu.CompilerParams(
            dimension_semantics=("arbitrary", "arbitrary"),
            vmem_limit_bytes=VMEM_LIMIT),
        name="modulations",
    )(cond, mod_w, mod_b.reshape(DEPTH, 1, N_MOD * D))


def _cond_row(i, tm, tokens_per_cond, first_row):
    if tokens_per_cond is None:
        return first_row
    return first_row + i // (tokens_per_cond // tm)


def _mod_spec(layer, which, tm, tokens_per_cond, first_row):
    return pl.BlockSpec(
        (None, None, 1, 3 * D),
        lambda i: (layer, _cond_row(i, tm, tokens_per_cond, first_row), 0, which))


def _const_spec(shape, index):
    return pl.BlockSpec(shape, lambda i: index, pipeline_mode=pl.Buffered(1))


def _ffn_kernel(*refs, pre_outproj, side=None):
    if pre_outproj:
        (x_ref, mix_ref, wout_ref, modp_ref, mod_ref, g_ref, wi_ref, wo_ref,
         o_ref, a_ref, h_ref) = refs
        x1_ref = o_ref
    else:
        x_ref, mod_ref, g_ref, wi_ref, wo_ref, o_ref, a_ref, h_ref = refs
        x1_ref = x_ref
    n_sub = x_ref.shape[0] // FFN_SUB

    def rows(s):
        return slice(s * FFN_SUB, (s + 1) * FFN_SUB)

    def outproj(s):
        if pre_outproj:
            x1_ref[rows(s), :] = x_ref[rows(s), :] + modp_ref[:, 2 * D:3 * D] * _dot(
                mix_ref[rows(s), :], wout_ref[...])

    def prologue(s, piece):
        r0 = s * FFN_SUB + piece * FFN_PIECE
        p0 = piece * FFN_PIECE
        x = x1_ref[r0:r0 + FFN_PIECE, :]
        h = _rms_mod(x, g_ref[...], mod_ref[:, D:2 * D], mod_ref[:, 0:D]).astype(BF16)
        h_ref[s, p0:p0 + FFN_PIECE, :] = h
        token = h[0:16, :]
        for r in range(16, FFN_PIECE, 16):
            token = token + h[r:r + 16, :]
        return functools.reduce(lambda a, b: a + b, [token[:, l:l + 128] for l in range(0, D, 128)])

    def anchor(s, token):
        zero = jnp.zeros_like(token)
        h_ref[s, 0:16, 0:128] = h_ref[s, 0:16, 0:128] + jnp.maximum(jnp.minimum(token, zero), zero)

    n_piece = FFN_SUB // FFN_PIECE
    outproj(0)
    for piece in range(n_piece):
        prologue(0, piece)
    n_chunk = -(-D_FF // FFN_CHUNK)
    for s in range(n_sub):
        items = [] if side is None else side[s]
        per_chunk = -(-len(items) // n_chunk)
        if s + 1 < n_sub:
            outproj(s + 1)
        for c, c0 in enumerate(range(0, D_FF, FFN_CHUNK)):
            c1 = min(c0 + FFN_CHUNK, D_FF)
            gate = _dot(h_ref[s], wi_ref[:, c0:c1])
            up = _dot(h_ref[s], wi_ref[:, D_FF + c0:D_FF + c1])
            a_ref[s, :, c0:c1] = (gate * jax.nn.sigmoid(gate) * up).astype(BF16)
            if s + 1 < n_sub and c < n_piece:
                anchor(s, prologue(s + 1, c))
            for item in items[c * per_chunk:(c + 1) * per_chunk]:
                item()
        o_ref[rows(s), :] = x1_ref[rows(s), :] + (0.5 * mod_ref[:, 2 * D:3 * D]) * _dot(
            a_ref[s], wo_ref[...])


def _with_casts(body, n_in, n_out, n_cast):
    if n_cast == 0:
        return body

    def kernel(*refs):
        ins, srcs = refs[:n_in], refs[n_in:n_in + n_cast]
        outs = refs[n_in + n_cast:n_in + n_cast + n_out]
        dsts = refs[n_in + n_cast + n_out:n_in + 2 * n_cast + n_out]
        for src, dst in zip(srcs, dsts):
            dst[...] = src[...].astype(BF16)
        body(*ins, *outs, *refs[n_in + 2 * n_cast + n_out:])

    return kernel


def _cast_specs(casts, steps):
    in_specs, out_specs, out_shapes, args = [], [], [], []
    for w, layer in casts:
        _, r, c = w.shape
        rows = r // steps
        assert r % steps == 0 and rows % 16 == 0
        in_specs.append(pl.BlockSpec((None, rows, c), lambda i, layer=layer: (layer, i, 0)))
        out_specs.append(pl.BlockSpec((rows, c), lambda i: (i, 0)))
        out_shapes.append(jax.ShapeDtypeStruct((r, c), BF16))
        args.append(w)
    return in_specs, out_specs, out_shapes, args


def _ffn(x, mods, norm_g, wi, wo, layer, which, tokens_per_cond, first_row,
         mix=None, w_out=None, casts=()):
    t = x.shape[0]
    tm = TM_FFN
    tok = pl.BlockSpec((tm, D), lambda i: (i, 0))
    in_specs = [tok]
    args = [x]
    if mix is not None:
        in_specs += [tok, _const_spec((MIX_WIDTH, D), (0, 0)),
                     _mod_spec(layer, 1, tm, tokens_per_cond, first_row)]
        args += [mix, w_out, mods]
    in_specs += [
        _mod_spec(layer, which, tm, tokens_per_cond, first_row),
        _const_spec((None, None, 1, D), (layer, which, 0, 0)),
        _const_spec((D, 2 * D_FF), (0, 0)),
        _const_spec((D_FF, D), (0, 0)),
    ]
    args += [mods, norm_g, wi, wo]
    c_in, c_out, c_shapes, c_args = _cast_specs(casts, t // tm)
    return pl.pallas_call(
        _with_casts(functools.partial(_ffn_kernel, pre_outproj=mix is not None),
                    len(in_specs), 1, len(casts)),
        grid=(t // tm,),
        in_specs=in_specs + c_in,
        out_specs=[tok] + c_out,
        out_shape=[jax.ShapeDtypeStruct((t, D), F32)] + c_shapes,
        scratch_shapes=[pltpu.VMEM((tm // FFN_SUB, FFN_SUB, D_FF), BF16),
                        pltpu.VMEM((tm // FFN_SUB, FFN_SUB, D), BF16)],
        compiler_params=pltpu.CompilerParams(
            dimension_semantics=("arbitrary",), vmem_limit_bytes=VMEM_LIMIT),
        name="ffn_outproj" if mix is not None else "ffn",
    )(*args, *c_args)


def _head_norm(x, m, g):
    ms = _dot((x * x).astype(BF16), m)
    return x * lax.rsqrt(ms + EPS) * g


def _rope(x, c, s_up, s_dn):
    n = x.shape[1]
    return x * c + pltpu.roll(x, n - 16, 1) * s_up + pltpu.roll(x, 16, 1) * s_dn


def _proj_kernel(*refs, rope):
    if rope:
        (x_ref, mod_ref, g_ref, win_ref, mq_ref, mk_ref, qg_ref, kg_ref,
         c_ref, su_ref, sd_ref, pool_ref, glu_ref, q_ref, k_ref, v_ref) = refs
    else:
        (x_ref, mod_ref, g_ref, win_ref, mq_ref, mk_ref, qg_ref, kg_ref,
         pool_ref, glu_ref, q_ref, k_ref, v_ref) = refs
    for r0 in range(0, x_ref.shape[0], PROJ_SUB):
        rows = slice(r0, r0 + PROJ_SUB)
        h = _rms_mod(x_ref[rows, :], g_ref[...], mod_ref[:, D:2 * D], mod_ref[:, 0:D]).astype(BF16)
        u = _dot(h, win_ref[...])
        pool_ref[rows, :] = u[:, :POOL_WIDTH]
        glu_ref[rows, :] = (u[:, POOL_WIDTH:POOL_WIDTH + CONV_WIDTH]
                            * jax.nn.sigmoid(u[:, POOL_WIDTH + CONV_WIDTH:ATTN_OFFSET]))
        q = _head_norm(u[:, ATTN_OFFSET:ATTN_OFFSET + ATTN_WIDTH], mq_ref[...], qg_ref[...])
        k = _head_norm(u[:, ATTN_OFFSET + ATTN_WIDTH:ATTN_OFFSET + ATTN_WIDTH + KV_WIDTH],
                       mk_ref[...], kg_ref[...])
        if rope:
            c, su, sd = c_ref[rows, :], su_ref[rows, :], sd_ref[rows, :]
            k = _rope(k, c, su, sd)
            rep = ATTN_WIDTH // KV_WIDTH
            q = _rope(q, jnp.concatenate([c] * rep, axis=1), jnp.concatenate([su] * rep, axis=1),
                      jnp.concatenate([sd] * rep, axis=1))
        q_ref[rows, :] = q.astype(BF16)
        k_ref[rows, :] = k
        v_ref[rows, :] = u[:, ATTN_OFFSET + ATTN_WIDTH + KV_WIDTH:]


def _proj(x, mods, norm_g, w_in, mq, mk, qg, kg, layer, tokens_per_cond, first_row,
          rope_tables=None, casts=()):
    t = x.shape[0]
    tm = TM_PROJ_ROPE if rope_tables is not None else TM_PROJ
    in_specs = [
        pl.BlockSpec((tm, D), lambda i: (i, 0)),
        _mod_spec(layer, 1, tm, tokens_per_cond, first_row),
        _const_spec((None, None, 1, D), (layer, 1, 0, 0)),
        _const_spec((D, IN_WIDTH), (0, 0)),
        _const_spec((ATTN_WIDTH, ATTN_WIDTH), (0, 0)),
        _const_spec((KV_WIDTH, KV_WIDTH), (0, 0)),
        _const_spec((None, 1, ATTN_WIDTH), (layer, 0, 0)),
        _const_spec((None, 1, KV_WIDTH), (layer, 0, 0)),
    ]
    args = [x, mods, norm_g, w_in, mq, mk, qg, kg]
    if rope_tables is not None:
        seq_tiles = rope_tables[0].shape[0] // tm
        in_specs += [pl.BlockSpec((tm, KV_WIDTH), lambda i: (i % seq_tiles, 0))] * 3
        args += list(rope_tables)

    def out(width):
        return pl.BlockSpec((tm, width), lambda i: (i, 0))

    c_in, c_out, c_shapes, c_args = _cast_specs(casts, t // tm)
    out_specs = [out(POOL_WIDTH), out(CONV_WIDTH), out(ATTN_WIDTH), out(KV_WIDTH), out(KV_WIDTH)]
    return pl.pallas_call(
        _with_casts(functools.partial(_proj_kernel, rope=rope_tables is not None),
                    len(in_specs), len(out_specs), len(casts)),
        grid=(t // tm,),
        in_specs=in_specs + c_in,
        out_specs=out_specs + c_out,
        out_shape=[
            jax.ShapeDtypeStruct((t, POOL_WIDTH), F32),
            jax.ShapeDtypeStruct((t, CONV_WIDTH), F32),
            jax.ShapeDtypeStruct((t, ATTN_WIDTH), BF16),
            jax.ShapeDtypeStruct((t, KV_WIDTH), F32),
            jax.ShapeDtypeStruct((t, KV_WIDTH), F32),
        ] + c_shapes,
        compiler_params=pltpu.CompilerParams(
            dimension_semantics=("arbitrary",), vmem_limit_bytes=VMEM_LIMIT),
        name="proj_rope" if rope_tables is not None else "proj",
    )(*args, *c_args)


def _fill_ext(ext_ref, prev_ref, cur_ref, next_ref, i, nb):
    ext_ref[0:HALO, :] = jnp.where(i > 0, prev_ref[...], 0.0)
    ext_ref[HALO:HALO + TB, :] = cur_ref[...]
    ext_ref[HALO + TB:, :] = jnp.where(i < nb - 1, next_ref[...], 0.0)


def _shift_copies(ext_ref, z_ref, shifts, lanes):
    for b in shifts:
        z_ref[b, :, lanes] = ext_ref[b:b + ZROWS, lanes]


def _window(z_ref, ext_off, r0, lanes):
    a, b = divmod(ext_off, 8)
    return z_ref[b, 8 * a + r0:8 * a + r0 + ROW_CHUNK, lanes]


def _pool_chunk(z_ref, pooled_ref, r0, i, n):
    lo_lanes, hi_lanes = slice(0, 128), slice(128, 256)
    first = lax.broadcasted_iota(jnp.int32, (ROW_CHUNK, 128), 1) < POOL_GROUP_DIM
    t = i * TB + r0 + lax.broadcasted_iota(jnp.int32, (ROW_CHUNK, 128), 0)

    def centred(total, half, cur):
        count = jnp.minimum(t + half, n) - jnp.maximum(t - half, 0)
        return (total / count.astype(F32) - cur).astype(BF16)

    def u(off, lanes):
        return _window(z_ref, HALO + off, r0, lanes)

    cur = u(0, lo_lanes)
    acc2 = u(-1, lo_lanes) + cur
    acc4 = acc2 + u(-2, lo_lanes) + u(1, lo_lanes)
    pooled_ref[r0:r0 + ROW_CHUNK, lo_lanes] = centred(
        jnp.where(first, acc2, acc4), jnp.where(first, 1, 2), cur)
    cur = u(0, hi_lanes)
    acc8 = cur
    for off in (-4, -3, -2, -1, 1, 2, 3):
        acc8 = acc8 + u(off, hi_lanes)
    acc16 = acc8
    for off in (-8, -7, -6, -5, 4, 5, 6, 7):
        acc16 = acc16 + u(off, hi_lanes)
    pooled_ref[r0:r0 + ROW_CHUNK, hi_lanes] = centred(
        jnp.where(first, acc8, acc16), jnp.where(first, 4, 8), cur)


def _conv_chunk(z_ref, act_ref, r0, dw_ref, cb_ref, cg_ref):
    acc = None
    for k in range(CONV_TAPS):
        term = _window(z_ref, HALO - CONV_TAPS // 2 + k, r0, slice(None)) * dw_ref[k:k + 1, :]
        acc = term if acc is None else acc + term
    y = acc + cb_ref[...]
    ms = jnp.mean(y * y, axis=-1, keepdims=True)
    z = y * lax.rsqrt(ms + EPS) * cg_ref[...]
    act_ref[r0:r0 + ROW_CHUNK, :] = (z * jax.nn.sigmoid(z)).astype(BF16)


def _att_scores(s_ref, q_ref, q0, j, kb, biases):
    lane = lax.broadcasted_iota(jnp.int32, (BLOCK, KV_WIDTH), 1)
    own = (lane // HEAD_DIM) == j
    qs = []
    for h in range(j * Q_GROUP, (j + 1) * Q_GROUP):
        c0 = (h // 2) * KV_WIDTH
        qh = q_ref[q0:q0 + BLOCK, c0:c0 + KV_WIDTH]
        if h % 2 != j:
            qh = jnp.concatenate([qh[:, HEAD_DIM:], qh[:, :HEAD_DIM]], axis=1)
        qs.append(jnp.where(own, qh, jnp.zeros_like(qh)))
    qj = jnp.concatenate(qs, axis=0)
    s = lax.dot_general(kb, qj, (((1,), (1,)), ((), ())), preferred_element_type=F32)
    if biases is None:
        s_ref[...] = s
    else:
        s_ref[0:BLOCK] = s[0:BLOCK] + biases[0]
        s_ref[BLOCK:2 * BLOCK] = s[BLOCK:2 * BLOCK]
        s_ref[2 * BLOCK:3 * BLOCK] = s[2 * BLOCK:3 * BLOCK] + biases[1]
        s_ref[3 * BLOCK:] = s[3 * BLOCK:]


def _att_weights(p_ref, s_ref, j, sink_ref, layer):
    sink = jnp.concatenate(
        [jnp.full((1, BLOCK), sink_ref[layer * N_HEADS + j * Q_GROUP + g] * LOG2E, F32)
         for g in range(Q_GROUP)], axis=1)
    m = jnp.maximum(jnp.max(s_ref[...], axis=0, keepdims=True), sink)
    for r in range(0, s_ref.shape[0], BLOCK):
        p_ref[r:r + BLOCK] = jnp.exp2(s_ref[r:r + BLOCK] - m).astype(BF16)
    return jnp.exp2(sink - m)


def _att_values(p_ref, sink_w, j, vt):
    v_ones = jnp.concatenate(
        [vt[j * HEAD_DIM:(j + 1) * HEAD_DIM, :], jnp.ones((16, vt.shape[1]), BF16)], axis=0)
    o = _dot(v_ones, p_ref[...])
    return o[0:HEAD_DIM, :] / (o[HEAD_DIM:HEAD_DIM + 1, :] + sink_w)


N_ATT_STAGES = (TB // BLOCK) * N_KV_HEADS + 2


def _mixer_parts(refs, layer, latent, n, i):
    nb = n // TB
    (pp_ref, pc_ref, pn_ref, gp_ref, gc_ref, gn_ref, q_ref) = refs[:7]
    rest = refs[7:]
    if latent:
        (kp_ref, kc_ref, kn_ref, vp_ref, vc_ref, vn_ref, ck_ref, cv_ref) = rest[:8]
        rest = rest[8:]
    else:
        (ka_ref, va_ref) = rest[:2]
        rest = rest[2:]
    (poolw_ref, pscale_ref, dw_ref, cb_ref, cg_ref, pw_ref, sink_ref,
     o_ref, pext_ref, gext_ref, zp_ref, zg_ref, pooled_ref, act_ref,
     s_ref, p_ref, at_ref) = rest

    operands = []

    def prep():
        _fill_ext(pext_ref, pp_ref, pc_ref, pn_ref, i, nb)
        _shift_copies(pext_ref, zp_ref, (6, 7, 0, 1), slice(0, 128))
        _shift_copies(pext_ref, zp_ref, range(8), slice(128, 256))
        _fill_ext(gext_ref, gp_ref, gc_ref, gn_ref, i, nb)
        _shift_copies(gext_ref, zg_ref, range(8), slice(None))
        if latent:
            k_seq = jnp.concatenate([kp_ref[...], kc_ref[...], kn_ref[...]], axis=0).astype(BF16)
            vt_seq = jnp.concatenate(
                [vp_ref[...], vc_ref[...], vn_ref[...]], axis=0).T.astype(BF16)
            k_ctx, vt_ctx = ck_ref[...].astype(BF16), cv_ref[...].T.astype(BF16)
            cols = Q_GROUP * BLOCK
            key = lax.broadcasted_iota(jnp.int32, (BLOCK, cols), 0)
            r = lax.broadcasted_iota(jnp.int32, (BLOCK, cols), 1) % BLOCK
        else:
            k_all, vt_all = ka_ref[...].astype(BF16), va_ref[...].T.astype(BF16)
        for qb in range(TB // BLOCK):
            q0 = qb * BLOCK
            biases = None
            if latent:
                k_all = jnp.concatenate([k_seq[q0:q0 + 3 * BLOCK], k_ctx], axis=0)
                vt_all = jnp.concatenate([vt_seq[:, q0:q0 + 3 * BLOCK], vt_ctx], axis=1)
                blk = i * (TB // BLOCK) + qb
                biases = (jnp.where((key >= r) & (blk > 0), 0.0, NEG_INF).astype(F32),
                          jnp.where((key <= r) & (blk < n // BLOCK - 1), 0.0, NEG_INF).astype(F32))
            operands.append((q0, k_all, vt_all, biases))

    vpu_chunks = []
    for r0 in range(0, TB, ROW_CHUNK):
        vpu_chunks.append(functools.partial(_pool_chunk, zp_ref, pooled_ref, r0, i, n))
        vpu_chunks.append(functools.partial(_conv_chunk, zg_ref, act_ref, r0, dw_ref, cb_ref, cg_ref))

    def pool_conv_matmuls():
        o_ref[:, 0:POOL_WIDTH] = (
            _dot(pooled_ref[...], poolw_ref[...]) * pscale_ref[...]).astype(BF16)
        o_ref[:, POOL_WIDTH:POOL_WIDTH + CONV_WIDTH] = _dot(act_ref[...], pw_ref[...]).astype(BF16)

    units = [(qb, j) for qb in range(TB // BLOCK) for j in range(N_KV_HEADS)]
    sink_w = {}

    def att_stage(t):
        if t < len(units):
            qb, j = units[t]
            q0, k_all, _, biases = operands[qb]
            _att_scores(s_ref.at[t % 2], q_ref, q0, j, k_all, biases)
        if 0 <= t - 1 < len(units):
            u = t - 1
            sink_w[u] = _att_weights(p_ref.at[u % 2], s_ref.at[u % 2], units[u][1], sink_ref, layer)
        if 0 <= t - 2 < len(units):
            u = t - 2
            qb, j = units[u]
            o = _att_values(p_ref.at[u % 2], sink_w.pop(u), j, operands[qb][2])
            for g in range(Q_GROUP):
                c0 = (j * Q_GROUP + g) * HEAD_DIM
                at_ref[qb, c0:c0 + HEAD_DIM, :] = o[:, g * BLOCK:(g + 1) * BLOCK]

    def att_store():
        for qb in range(TB // BLOCK):
            o_ref[qb * BLOCK:(qb + 1) * BLOCK, POOL_WIDTH + CONV_WIDTH:] = (
                at_ref[qb].T.astype(BF16))

    return prep, att_stage, vpu_chunks, pool_conv_matmuls, att_store


def _mixer_items(parts):
    prep, att_stage, vpu_chunks, pool_conv_matmuls, att_store = parts
    per_stage = -(-len(vpu_chunks) // N_ATT_STAGES)

    def stage(t):
        att_stage(t)
        for chunk in vpu_chunks[t * per_stage:(t + 1) * per_stage]:
            chunk()

    def finish():
        pool_conv_matmuls()
        att_store()

    return [prep] + [functools.partial(stage, t) for t in range(N_ATT_STAGES)] + [finish]


def _mixer_kernel(*refs, layer, latent, n):
    prep, att_stage, vpu_chunks, pool_conv_matmuls, att_store = _mixer_parts(
        refs, layer, latent, n, pl.program_id(1))
    prep()
    if latent:
        for chunk in vpu_chunks:
            chunk()
        pool_conv_matmuls()
        for t in range(N_ATT_STAGES):
            att_stage(t)
    else:
        per_stage = -(-len(vpu_chunks) // N_ATT_STAGES)
        for t in range(N_ATT_STAGES):
            att_stage(t)
            for chunk in vpu_chunks[t * per_stage:(t + 1) * per_stage]:
                chunk()
        pool_conv_matmuls()
    att_store()


def _mixer(pool_in, glu, q, k, v, wts, layer, cache=None):
    b, n, _ = pool_in.shape
    nb = n // TB
    hpb = TB // HALO
    nh = n // HALO
    kpb = TB // BLOCK
    nk = n // BLOCK
    n_keys = n if cache is None else 3 * BLOCK + cache[0].shape[2]

    def cur(width):
        return pl.BlockSpec((None, TB, width), lambda bi, i: (bi, i, 0))

    def halo_prev(width):
        return pl.BlockSpec((None, HALO, width), lambda bi, i: (bi, jnp.maximum(i * hpb - 1, 0), 0))

    def halo_next(width):
        return pl.BlockSpec((None, HALO, width),
                            lambda bi, i: (bi, jnp.minimum((i + 1) * hpb, nh - 1), 0))

    in_specs = [halo_prev(POOL_WIDTH), cur(POOL_WIDTH), halo_next(POOL_WIDTH),
                halo_prev(CONV_WIDTH), cur(CONV_WIDTH), halo_next(CONV_WIDTH),
                cur(ATTN_WIDTH)]
    args = [pool_in, pool_in, pool_in, glu, glu, glu, q]
    if cache is not None:
        cache_k, cache_v = cache
        past = cache_k.shape[2]
        blk_prev = pl.BlockSpec((None, BLOCK, KV_WIDTH),
                                lambda bi, i: (bi, jnp.maximum(i * kpb - 1, 0), 0))
        blk_next = pl.BlockSpec((None, BLOCK, KV_WIDTH),
                                lambda bi, i: (bi, jnp.minimum((i + 1) * kpb, nk - 1), 0))
        cspec = pl.BlockSpec((None, None, past, KV_WIDTH), lambda bi, i: (bi, layer, 0, 0))
        in_specs += [blk_prev, cur(KV_WIDTH), blk_next, blk_prev, cur(KV_WIDTH), blk_next,
                     cspec, cspec]
        args += [k, k, k, v, v, v, cache_k, cache_v]
    else:
        whole = pl.BlockSpec((None, n, KV_WIDTH), lambda bi, i: (bi, 0, 0))
        in_specs += [whole, whole]
        args += [k, v]
    in_specs += _mixer_weight_specs(layer)
    args += _mixer_weight_args(wts)
    return pl.pallas_call(
        functools.partial(_mixer_kernel, layer=layer, latent=cache is not None, n=n),
        grid=(b, nb),
        in_specs=in_specs,
        out_specs=pl.BlockSpec((None, TB, MIX_WIDTH), lambda bi, i: (bi, i, 0)),
        out_shape=jax.ShapeDtypeStruct((b, n, MIX_WIDTH), BF16),
        scratch_shapes=_mixer_scratch(n_keys),
        compiler_params=pltpu.CompilerParams(
            dimension_semantics=("arbitrary", "arbitrary"), vmem_limit_bytes=VMEM_LIMIT),
        name="mixer_latent" if cache is not None else "mixer_context",
    )(*args)


def _mixer_weight_specs(layer):
    def const(shape, index):
        return pl.BlockSpec(shape, lambda *_: index)

    return [
        const((None, POOL_WIDTH, POOL_WIDTH), (layer, 0, 0)),
        const((None, 1, POOL_WIDTH), (layer, 0, 0)),
        const((None, CONV_TAPS, CONV_WIDTH), (layer, 0, 0)),
        const((None, 1, CONV_WIDTH), (layer, 0, 0)),
        const((None, 1, CONV_WIDTH), (layer, 0, 0)),
        const((None, CONV_WIDTH, CONV_WIDTH), (layer, 0, 0)),
        pl.BlockSpec(memory_space=pltpu.SMEM),
    ]


def _mixer_weight_args(wts):
    return [wts["pool_bd"], wts["pool_scale"], wts["conv_dw"], wts["conv_b"],
            wts["conv_norm_g"], wts["conv_pw"], wts["sink"]]


def _mixer_scratch(n_keys):
    return [pltpu.VMEM((TB + 2 * HALO, POOL_WIDTH), F32),
            pltpu.VMEM((TB + 2 * HALO, CONV_WIDTH), F32),
            pltpu.VMEM((8, ZROWS, POOL_WIDTH), F32),
            pltpu.VMEM((8, ZROWS, CONV_WIDTH), F32),
            pltpu.VMEM((TB, POOL_WIDTH), BF16),
            pltpu.VMEM((TB, CONV_WIDTH), BF16),
            pltpu.VMEM((2, n_keys, Q_GROUP * BLOCK), F32),
            pltpu.VMEM((2, n_keys, Q_GROUP * BLOCK), BF16),
            pltpu.VMEM((TB // BLOCK, ATTN_WIDTH, BLOCK), F32)]


N_CTX_BLOCK_OPERANDS = 9


def _ffn_side_kernel(*refs, n_ffn_in, n_mix_in, n_ffn_scratch, layer, n):
    ffn_in = refs[:n_ffn_in]
    mix_in = refs[n_ffn_in:n_ffn_in + n_mix_in]
    o_ref, mix_out = refs[n_ffn_in + n_mix_in:n_ffn_in + n_mix_in + 2]
    scratch = refs[n_ffn_in + n_mix_in + 2:]
    ffn_scratch, mix_scratch = scratch[:n_ffn_scratch], scratch[n_ffn_scratch:]
    side = []
    for s in range(mix_out.shape[0]):
        block = (tuple(r.at[s] for r in mix_in[:N_CTX_BLOCK_OPERANDS])
                 + tuple(mix_in[N_CTX_BLOCK_OPERANDS:]) + (mix_out.at[s],) + tuple(mix_scratch))
        side.append(_mixer_items(_mixer_parts(block, layer, False, n, 0)))
    _ffn_kernel(*ffn_in, o_ref, *ffn_scratch, pre_outproj=False, side=side)


def _ffn_with_context_mixer(x, mods, norm_g, wi, wo, layer, which, tokens_per_cond, first_row,
                            pool_in, glu, q, k, v, wts, casts=()):
    t = x.shape[0]
    tm = TM_FFN
    b, n, _ = pool_in.shape
    per_step = tm // FFN_SUB
    assert n == TB and b == (t // tm) * per_step
    tok = pl.BlockSpec((tm, D), lambda i: (i, 0))
    ffn_specs = [
        tok,
        _mod_spec(layer, which, tm, tokens_per_cond, first_row),
        _const_spec((None, None, 1, D), (layer, which, 0, 0)),
        _const_spec((D, 2 * D_FF), (0, 0)),
        _const_spec((D_FF, D), (0, 0)),
    ]
    ffn_args = [x, mods, norm_g, wi, wo]

    def seqs(rows, width):
        return pl.BlockSpec((per_step, rows, width), lambda i: (i, 0, 0))

    mix_specs = [seqs(HALO, POOL_WIDTH), seqs(TB, POOL_WIDTH), seqs(HALO, POOL_WIDTH),
                 seqs(HALO, CONV_WIDTH), seqs(TB, CONV_WIDTH), seqs(HALO, CONV_WIDTH),
                 seqs(TB, ATTN_WIDTH), seqs(TB, KV_WIDTH), seqs(TB, KV_WIDTH)]
    mix_args = [pool_in, pool_in, pool_in, glu, glu, glu, q, k, v]
    assert len(mix_specs) == N_CTX_BLOCK_OPERANDS
    mix_specs += _mixer_weight_specs(layer)
    mix_args += _mixer_weight_args(wts)
    ffn_scratch = [pltpu.VMEM((tm // FFN_SUB, FFN_SUB, D_FF), BF16),
                   pltpu.VMEM((tm // FFN_SUB, FFN_SUB, D), BF16)]
    c_in, c_out, c_shapes, c_args = _cast_specs(casts, t // tm)
    body = functools.partial(_ffn_side_kernel, n_ffn_in=len(ffn_specs), n_mix_in=len(mix_specs),
                             n_ffn_scratch=len(ffn_scratch), layer=layer, n=n)
    return pl.pallas_call(
        _with_casts(body, len(ffn_specs) + len(mix_specs), 2, len(casts)),
        grid=(t // tm,),
        in_specs=ffn_specs + mix_specs + c_in,
        out_specs=[tok, pl.BlockSpec((per_step, TB, MIX_WIDTH), lambda i: (i, 0, 0))] + c_out,
        out_shape=[jax.ShapeDtypeStruct((t, D), F32),
                   jax.ShapeDtypeStruct((b, n, MIX_WIDTH), BF16)] + c_shapes,
        scratch_shapes=ffn_scratch + _mixer_scratch(n),
        compiler_params=pltpu.CompilerParams(
            dimension_semantics=("arbitrary",), vmem_limit_bytes=VMEM_LIMIT),
        name="ffn_with_context_mixer",
    )(*ffn_args, *mix_args, *c_args)


def _rope_tables(n):
    t = np.arange(n)
    half = HEAD_DIM // 2
    inv = ROPE_THETA ** (-np.arange(0, half, 2, dtype=np.float64) / half)
    zeros = np.zeros((n, half // 2))
    cos, s_up, s_dn = [], [], []
    for pos in (t // GRID_W, t % GRID_W):
        ang = pos[:, None].astype(np.float64) * inv[None, :]
        cos += [np.cos(ang), np.cos(ang)]
        s_up += [-np.sin(ang), zeros]
        s_dn += [zeros, np.sin(ang)]
    return tuple(jnp.asarray(np.concatenate(parts * N_KV_HEADS, axis=1), dtype=F32)
                 for parts in (cos, s_up, s_dn))


def _block_diag_mean(width):
    idx = jnp.arange(width) // HEAD_DIM
    return jnp.where(idx[:, None] == idx[None, :], 1.0 / HEAD_DIM, 0.0).astype(BF16)


def kernel(x_prompt, x_sample, cache_k, cache_v, c, c_ctx, mod_w, mod_b, norm_g, ffn1_wi, ffn1_wo, ffn2_wi, ffn2_wo, w_in, w_out, pool_w, pool_scale, conv_dw, conv_b, conv_norm_g, conv_pw, q_norm_g, k_norm_g, sink):
    batch, seq, _ = x_prompt.shape
    dec_batch, dec_seq, _ = x_sample.shape
    past = cache_k.shape[2]
    assert 1 + dec_batch <= COND_ROWS

    cond = jnp.zeros((COND_ROWS, D), F32).at[0].set(c_ctx).at[1:1 + dec_batch].set(c)
    mods = _modulations(cond, mod_w, mod_b).reshape(DEPTH, COND_ROWS, 1, N_MOD * D)

    norm_g4 = norm_g.reshape(DEPTH, 3, 1, D)
    wi1, wo1 = ffn1_wi[0].astype(BF16), ffn1_wo[0].astype(BF16)
    wi2 = wo2 = w_in_b = w_out_b = None
    eye = jnp.eye(POOL_GROUPS, dtype=F32)
    pool_bd = (eye[None, :, None, :, None] * pool_w[:, :, :, None, :]).reshape(
        DEPTH, POOL_WIDTH, POOL_WIDTH).astype(BF16)
    wts = {
        "pool_bd": pool_bd,
        "pool_scale": pool_scale.reshape(DEPTH, 1, POOL_WIDTH),
        "conv_dw": conv_dw,
        "conv_b": conv_b.reshape(DEPTH, 1, CONV_WIDTH),
        "conv_norm_g": conv_norm_g.reshape(DEPTH, 1, CONV_WIDTH),
        "conv_pw": conv_pw.astype(BF16),
        "sink": sink.reshape(DEPTH * N_HEADS),
    }
    mq, mk = _block_diag_mean(ATTN_WIDTH), _block_diag_mean(KV_WIDTH)
    qg = jnp.tile(q_norm_g, (1, N_HEADS)).reshape(DEPTH, 1, ATTN_WIDTH) * (SCALE * LOG2E)
    kg = jnp.tile(k_norm_g, (1, N_KV_HEADS)).reshape(DEPTH, 1, KV_WIDTH)
    rope = _rope_tables(dec_seq)
    cache_k4 = cache_k.reshape(dec_batch, DEPTH, past, KV_WIDTH)
    cache_v4 = cache_v.reshape(dec_batch, DEPTH, past, KV_WIDTH)

    def seqs(a, bsz, n):
        return a.reshape(bsz, n, a.shape[-1])

    yp = x_prompt.reshape(batch * seq, D)
    ys = x_sample.reshape(dec_batch * dec_seq, D)
    ks, vs = [], []
    for layer in range(DEPTH):
        def nxt(*ws):
            return [(w, layer + 1) for w in ws] if layer + 1 < DEPTH else []

        def first(*ws):
            return [(w, 0) for w in ws] if layer == 0 else []

        yp, *now = _ffn(yp, mods, norm_g4, wi1, wo1, layer, 0, None, 0,
                        casts=first(w_in, w_out, ffn2_wo))
        if layer == 0:
            w_in_b, w_out_b, wo2 = now
        *c_parts, = _proj(yp, mods, norm_g4, w_in_b, mq, mk, qg, kg, layer, None, 0,
                          casts=first(ffn2_wi))
        if layer == 0:
            wi2 = c_parts.pop()
        ys, mix_c, *wo1_n = _ffn_with_context_mixer(
            ys, mods, norm_g4, wi1, wo1, layer, 0, dec_seq, 1,
            *(seqs(a, batch, seq) for a in c_parts), wts, casts=nxt(ffn1_wo))
        l_parts = _proj(ys, mods, norm_g4, w_in_b, mq, mk, qg, kg, layer, dec_seq, 1,
                        rope_tables=rope)
        yp, *wi2_n = _ffn(yp, mods, norm_g4, wi2, wo2, layer, 2, None, 0,
                          mix=mix_c.reshape(batch * seq, MIX_WIDTH), w_out=w_out_b,
                          casts=nxt(ffn2_wi))
        mix_l = _mixer(*(seqs(a, dec_batch, dec_seq) for a in l_parts), wts, layer,
                       cache=(cache_k4, cache_v4))
        ys, *rest_n = _ffn(ys, mods, norm_g4, wi2, wo2, layer, 2, dec_seq, 1,
                           mix=mix_l.reshape(dec_batch * dec_seq, MIX_WIDTH), w_out=w_out_b,
                           casts=nxt(ffn2_wo, w_in, w_out, ffn1_wi))
        ks.append(c_parts[3].reshape(batch, seq, N_KV_HEADS, HEAD_DIM))
        vs.append(c_parts[4].reshape(batch, seq, N_KV_HEADS, HEAD_DIM))
        if layer + 1 < DEPTH:
            (wo1,), (wi2,), (wo2, w_in_b, w_out_b, wi1) = wo1_n, wi2_n, rest_n
    return (yp.reshape(batch, seq, D), ys.reshape(dec_batch, dec_seq, D),
            jnp.stack(ks, axis=1), jnp.stack(vs, axis=1))
```

```python
import functools

import jax
import jax.numpy as jnp
import numpy as np
from jax import lax
from jax.experimental import pallas as pl
from jax.experimental.pallas import tpu as pltpu

D = 1024
DEPTH = 2
GRID_W = 64
POOL_WIDTH = 256
POOL_GROUPS = 4
POOL_GROUP_DIM = POOL_WIDTH // POOL_GROUPS
CONV_WIDTH = 256
CONV_TAPS = 31
N_HEADS = 8
N_KV_HEADS = 2
HEAD_DIM = 64
Q_GROUP = N_HEADS // N_KV_HEADS
ATTN_WIDTH = N_HEADS * HEAD_DIM
KV_WIDTH = N_KV_HEADS * HEAD_DIM
MIX_WIDTH = POOL_WIDTH + CONV_WIDTH + ATTN_WIDTH
ATTN_OFFSET = POOL_WIDTH + 2 * CONV_WIDTH
IN_WIDTH = ATTN_OFFSET + ATTN_WIDTH + 2 * KV_WIDTH
BLOCK = 128
D_FF = 2816
N_MOD = 9
ROPE_THETA = 10000.0
EPS = 1e-6
NEG_INF = -1e30
SCALE = HEAD_DIM ** -0.5
LOG2E = 1.4426950408889634

COND_ROWS = 8
HALO = 16
TM_FFN = 1024
FFN_SUB = 512
FFN_PIECE = 128
TM_PROJ = 1024
PROJ_SUB = 256
MXU_TILE = 256
FFN_CHUNK = 2 * MXU_TILE
TB = 256
ROW_CHUNK = 64
ZROWS = TB + 24
VMEM_LIMIT = 56 * 1024 * 1024

F32 = jnp.float32
BF16 = jnp.bfloat16


def _dot(a, b):
    return jnp.dot(a, b, preferred_element_type=F32)


def _rms_mod(x, g, sc, sh):
    ms = jnp.mean(x * x, axis=-1, keepdims=True)
    return (x * lax.rsqrt(ms + EPS) * g) * (1.0 + sc) + sh


def _mod_kernel(cond_ref, w_ref, b_ref, o_ref):
    c = cond_ref[...]
    s = (c * jax.nn.sigmoid(c)).astype(BF16)
    o_ref[...] = _dot(s, w_ref[...].astype(BF16)) + b_ref[...]


def _modulations(cond, mod_w, mod_b):
    tn = 2304
    nt = (N_MOD * D) // tn
    return pl.pallas_call(
        _mod_kernel,
        grid=(DEPTH, nt),
        in_specs=[
            pl.BlockSpec((COND_ROWS, D), lambda l, j: (0, 0)),
            pl.BlockSpec((None, D, tn), lambda l, j: (l, 0, j)),
            pl.BlockSpec((None, 1, tn), lambda l, j: (l, 0, j)),
        ],
        out_specs=pl.BlockSpec((None, COND_ROWS, tn), lambda l, j: (l, 0, j)),
        out_shape=jax.ShapeDtypeStruct((DEPTH, COND_ROWS, N_MOD * D), F32),
        compiler_params=pltpu.CompilerParams(
            dimension_semantics=("arbitrary", "arbitrary"),
            vmem_limit_bytes=VMEM_LIMIT),
        name="modulations",
    )(cond, mod_w, mod_b.reshape(DEPTH, 1, N_MOD * D))


def _cond_row(i, tm, tokens_per_cond, first_row):
    if tokens_per_cond is None:
        return first_row
    return first_row + i // (tokens_per_cond // tm)


def _mod_spec(layer, which, tm, tokens_per_cond, first_row):
    return pl.BlockSpec(
        (None, None, 1, 3 * D),
        lambda i: (layer, _cond_row(i, tm, tokens_per_cond, first_row), 0, which))


def _const_spec(shape, index):
    return pl.BlockSpec(shape, lambda i: index, pipeline_mode=pl.Buffered(1))


def _ffn_kernel(*refs, pre_outproj, side=None):
    if pre_outproj:
        (x_ref, mix_ref, wout_ref, modp_ref, mod_ref, g_ref, wi_ref, wo_ref,
         o_ref, a_ref, h_ref) = refs
        x1_ref = o_ref
    else:
        x_ref, mod_ref, g_ref, wi_ref, wo_ref, o_ref, a_ref, h_ref = refs
        x1_ref = x_ref
    n_sub = x_ref.shape[0] // FFN_SUB

    def rows(s):
        return slice(s * FFN_SUB, (s + 1) * FFN_SUB)

    def outproj(s):
        if pre_outproj:
            x1_ref[rows(s), :] = x_ref[rows(s), :] + modp_ref[:, 2 * D:3 * D] * _dot(
                mix_ref[rows(s), :], wout_ref[...])

    def prologue(s, piece):
        r0 = s * FFN_SUB + piece * FFN_PIECE
        p0 = piece * FFN_PIECE
        x = x1_ref[r0:r0 + FFN_PIECE, :]
        h = _rms_mod(x, g_ref[...], mod_ref[:, D:2 * D], mod_ref[:, 0:D]).astype(BF16)
        h_ref[s, p0:p0 + FFN_PIECE, :] = h
        token = h[0:16, :]
        for r in range(16, FFN_PIECE, 16):
            token = token + h[r:r + 16, :]
        return functools.reduce(lambda a, b: a + b, [token[:, l:l + 128] for l in range(0, D, 128)])

    def anchor(s, token):
        zero = jnp.zeros_like(token)
        h_ref[s, 0:16, 0:128] = h_ref[s, 0:16, 0:128] + jnp.maximum(jnp.minimum(token, zero), zero)

    n_piece = FFN_SUB // FFN_PIECE
    outproj(0)
    for piece in range(n_piece):
        prologue(0, piece)
    n_chunk = -(-D_FF // FFN_CHUNK)
    for s in range(n_sub):
        items = [] if side is None else side[s]
        per_chunk = -(-len(items) // n_chunk)
        if s + 1 < n_sub:
            outproj(s + 1)
        for c, c0 in enumerate(range(0, D_FF, FFN_CHUNK)):
            c1 = min(c0 + FFN_CHUNK, D_FF)
            gate = _dot(h_ref[s], wi_ref[:, c0:c1])
            up = _dot(h_ref[s], wi_ref[:, D_FF + c0:D_FF + c1])
            a_ref[s, :, c0:c1] = (gate * jax.nn.sigmoid(gate) * up).astype(BF16)
            if s + 1 < n_sub and c < n_piece:
                anchor(s, prologue(s + 1, c))
            for item in items[c * per_chunk:(c + 1) * per_chunk]:
                item()
        o_ref[rows(s), :] = x1_ref[rows(s), :] + (0.5 * mod_ref[:, 2 * D:3 * D]) * _dot(
            a_ref[s], wo_ref[...])


def _with_casts(body, n_in, n_out, n_cast):
    if n_cast == 0:
        return body

    def kernel(*refs):
        ins, srcs = refs[:n_in], refs[n_in:n_in + n_cast]
        outs = refs[n_in + n_cast:n_in + n_cast + n_out]
        dsts = refs[n_in + n_cast + n_out:n_in + 2 * n_cast + n_out]
        for src, dst in zip(srcs, dsts):
            dst[...] = src[...].astype(BF16)
        body(*ins, *outs, *refs[n_in + 2 * n_cast + n_out:])

    return kernel


def _cast_specs(casts, steps):
    in_specs, out_specs, out_shapes, args = [], [], [], []
    for w, layer in casts:
        _, r, c = w.shape
        rows = r // steps
        assert r % steps == 0 and rows % 16 == 0
        in_specs.append(pl.BlockSpec((None, rows, c), lambda i, layer=layer: (layer, i, 0)))
        out_specs.append(pl.BlockSpec((rows, c), lambda i: (i, 0)))
        out_shapes.append(jax.ShapeDtypeStruct((r, c), BF16))
        args.append(w)
    return in_specs, out_specs, out_shapes, args


def _ffn(x, mods, norm_g, wi, wo, layer, which, tokens_per_cond, first_row,
         mix=None, w_out=None, casts=()):
    t = x.shape[0]
    tm = TM_FFN
    tok = pl.BlockSpec((tm, D), lambda i: (i, 0))
    in_specs = [tok]
    args = [x]
    if mix is not None:
        in_specs += [tok, _const_spec((MIX_WIDTH, D), (0, 0)),
                     _mod_spec(layer, 1, tm, tokens_per_cond, first_row)]
        args += [mix, w_out, mods]
    in_specs += [
        _mod_spec(layer, which, tm, tokens_per_cond, first_row),
        _const_spec((None, None, 1, D), (layer, which, 0, 0)),
        _const_spec((D, 2 * D_FF), (0, 0)),
        _const_spec((D_FF, D), (0, 0)),
    ]
    args += [mods, norm_g, wi, wo]
    c_in, c_out, c_shapes, c_args = _cast_specs(casts, t // tm)
    return pl.pallas_call(
        _with_casts(functools.partial(_ffn_kernel, pre_outproj=mix is not None),
                    len(in_specs), 1, len(casts)),
        grid=(t // tm,),
        in_specs=in_specs + c_in,
        out_specs=[tok] + c_out,
        out_shape=[jax.ShapeDtypeStruct((t, D), F32)] + c_shapes,
        scratch_shapes=[pltpu.VMEM((tm // FFN_SUB, FFN_SUB, D_FF), BF16),
                        pltpu.VMEM((tm // FFN_SUB, FFN_SUB, D), BF16)],
        compiler_params=pltpu.CompilerParams(
            dimension_semantics=("arbitrary",), vmem_limit_bytes=VMEM_LIMIT),
        name="ffn_outproj" if mix is not None else "ffn",
    )(*args, *c_args)


def _head_norm(x, m, g):
    w = m.shape[0]
    sq = (x * x).astype(BF16)
    ms = jnp.concatenate([_dot(sq[:, c:c + w], m) for c in range(0, x.shape[1], w)], axis=1)
    return x * lax.rsqrt(ms + EPS) * g


def _rope(x, c, s_up, s_dn):
    n = x.shape[1]
    return x * c + pltpu.roll(x, n - 16, 1) * s_up + pltpu.roll(x, 16, 1) * s_dn


def _proj_kernel(*refs, rope):
    if rope:
        (x_ref, mod_ref, g_ref, win_ref, mq_ref, mk_ref, qg_ref, kg_ref,
         c_ref, su_ref, sd_ref, pool_ref, glu_ref, q_ref, k_ref, v_ref) = refs
    else:
        (x_ref, mod_ref, g_ref, win_ref, mq_ref, mk_ref, qg_ref, kg_ref,
         pool_ref, glu_ref, q_ref, k_ref, v_ref) = refs
    for r0 in range(0, x_ref.shape[0], PROJ_SUB):
        rows = slice(r0, r0 + PROJ_SUB)
        h = _rms_mod(x_ref[rows, :], g_ref[...], mod_ref[:, D:2 * D], mod_ref[:, 0:D]).astype(BF16)
        u = _dot(h, win_ref[...])
        pool_ref[rows, :] = u[:, :POOL_WIDTH]
        glu_ref[rows, :] = (u[:, POOL_WIDTH:POOL_WIDTH + CONV_WIDTH]
                            * jax.nn.sigmoid(u[:, POOL_WIDTH + CONV_WIDTH:ATTN_OFFSET]))
        q = _head_norm(u[:, ATTN_OFFSET:ATTN_OFFSET + ATTN_WIDTH], mq_ref[...], qg_ref[...])
        k = _head_norm(u[:, ATTN_OFFSET + ATTN_WIDTH:ATTN_OFFSET + ATTN_WIDTH + KV_WIDTH],
                       mk_ref[...], kg_ref[...])
        if rope:
            c, su, sd = c_ref[rows, :], su_ref[rows, :], sd_ref[rows, :]
            k = _rope(k, c, su, sd)
            rep = ATTN_WIDTH // KV_WIDTH
            q = _rope(q, jnp.concatenate([c] * rep, axis=1), jnp.concatenate([su] * rep, axis=1),
                      jnp.concatenate([sd] * rep, axis=1))
        q_ref[rows, :] = q.astype(BF16)
        k_ref[rows, :] = k
        v_ref[rows, :] = u[:, ATTN_OFFSET + ATTN_WIDTH + KV_WIDTH:]


def _proj(x, mods, norm_g, w_in, mq, mk, qg, kg, layer, tokens_per_cond, first_row,
          rope_tables=None, casts=()):
    t = x.shape[0]
    tm = TM_PROJ
    in_specs = [
        pl.BlockSpec((tm, D), lambda i: (i, 0)),
        _mod_spec(layer, 1, tm, tokens_per_cond, first_row),
        _const_spec((None, None, 1, D), (layer, 1, 0, 0)),
        _const_spec((D, IN_WIDTH), (0, 0)),
        _const_spec(mq.shape, (0, 0)),
        _const_spec(mk.shape, (0, 0)),
        _const_spec((None, 1, ATTN_WIDTH), (layer, 0, 0)),
        _const_spec((None, 1, KV_WIDTH), (layer, 0, 0)),
    ]
    args = [x, mods, norm_g, w_in, mq, mk, qg, kg]
    if rope_tables is not None:
        seq_tiles = rope_tables[0].shape[0] // tm
        in_specs += [pl.BlockSpec((tm, KV_WIDTH), lambda i: (i % seq_tiles, 0))] * 3
        args += list(rope_tables)

    def out(width):
        return pl.BlockSpec((tm, width), lambda i: (i, 0))

    c_in, c_out, c_shapes, c_args = _cast_specs(casts, t // tm)
    out_specs = [out(POOL_WIDTH), out(CONV_WIDTH), out(ATTN_WIDTH), out(KV_WIDTH), out(KV_WIDTH)]
    return pl.pallas_call(
        _with_casts(functools.partial(_proj_kernel, rope=rope_tables is not None),
                    len(in_specs), len(out_specs), len(casts)),
        grid=(t // tm,),
        in_specs=in_specs + c_in,
        out_specs=out_specs + c_out,
        out_shape=[
            jax.ShapeDtypeStruct((t, POOL_WIDTH), F32),
            jax.ShapeDtypeStruct((t, CONV_WIDTH), F32),
            jax.ShapeDtypeStruct((t, ATTN_WIDTH), BF16),
            jax.ShapeDtypeStruct((t, KV_WIDTH), F32),
            jax.ShapeDtypeStruct((t, KV_WIDTH), F32),
        ] + c_shapes,
        compiler_params=pltpu.CompilerParams(
            dimension_semantics=("arbitrary",), vmem_limit_bytes=VMEM_LIMIT),
        name="proj_rope" if rope_tables is not None else "proj",
    )(*args, *c_args)


def _fill_ext(ext_ref, prev_ref, cur_ref, next_ref, i, nb):
    ext_ref[0:HALO, :] = jnp.where(i > 0, prev_ref[...], 0.0)
    ext_ref[HALO:HALO + TB, :] = cur_ref[...]
    ext_ref[HALO + TB:, :] = jnp.where(i < nb - 1, next_ref[...], 0.0)


def _shift_copies(ext_ref, z_ref, shifts, lanes):
    for b in shifts:
        z_ref[b, :, lanes] = ext_ref[b:b + ZROWS, lanes]


def _window(z_ref, ext_off, r0, lanes):
    a, b = divmod(ext_off, 8)
    return z_ref[b, 8 * a + r0:8 * a + r0 + ROW_CHUNK, lanes]


def _pool_chunk(z_ref, pooled_ref, r0, i, n):
    lo_lanes, hi_lanes = slice(0, 128), slice(128, 256)
    first = lax.broadcasted_iota(jnp.int32, (ROW_CHUNK, 128), 1) < POOL_GROUP_DIM
    t = i * TB + r0 + lax.broadcasted_iota(jnp.int32, (ROW_CHUNK, 128), 0)

    def centred(total, half, cur):
        count = jnp.minimum(t + half, n) - jnp.maximum(t - half, 0)
        return (total / count.astype(F32) - cur).astype(BF16)

    def u(off, lanes):
        return _window(z_ref, HALO + off, r0, lanes)

    cur = u(0, lo_lanes)
    acc2 = u(-1, lo_lanes) + cur
    acc4 = acc2 + u(-2, lo_lanes) + u(1, lo_lanes)
    pooled_ref[r0:r0 + ROW_CHUNK, lo_lanes] = centred(
        jnp.where(first, acc2, acc4), jnp.where(first, 1, 2), cur)
    cur = u(0, hi_lanes)
    acc8 = cur
    for off in (-4, -3, -2, -1, 1, 2, 3):
        acc8 = acc8 + u(off, hi_lanes)
    acc16 = acc8
    for off in (-8, -7, -6, -5, 4, 5, 6, 7):
        acc16 = acc16 + u(off, hi_lanes)
    pooled_ref[r0:r0 + ROW_CHUNK, hi_lanes] = centred(
        jnp.where(first, acc8, acc16), jnp.where(first, 4, 8), cur)


def _conv_chunk(z_ref, act_ref, r0, dw_ref, cb_ref, cg_ref):
    acc = None
    for k in range(CONV_TAPS):
        term = _window(z_ref, HALO - CONV_TAPS // 2 + k, r0, slice(None)) * dw_ref[k:k + 1, :]
        acc = term if acc is None else acc + term
    y = acc + cb_ref[...]
    ms = jnp.mean(y * y, axis=-1, keepdims=True)
    z = y * lax.rsqrt(ms + EPS) * cg_ref[...]
    act_ref[r0:r0 + ROW_CHUNK, :] = (z * jax.nn.sigmoid(z)).astype(BF16)


def _att_scores(s_ref, q_ref, q0, j, kb, biases):
    lane = lax.broadcasted_iota(jnp.int32, (BLOCK, KV_WIDTH), 1)
    own = (lane // HEAD_DIM) == j
    qs = []
    for h in range(j * Q_GROUP, (j + 1) * Q_GROUP):
        c0 = (h // 2) * KV_WIDTH
        qh = q_ref[q0:q0 + BLOCK, c0:c0 + KV_WIDTH]
        if h % 2 != j:
            qh = jnp.concatenate([qh[:, HEAD_DIM:], qh[:, :HEAD_DIM]], axis=1)
        qs.append(jnp.where(own, qh, jnp.zeros_like(qh)))
    qj = jnp.concatenate(qs, axis=0)
    s = lax.dot_general(kb, qj, (((1,), (1,)), ((), ())), preferred_element_type=F32)
    if biases is None:
        s_ref[...] = s
    else:
        s_ref[0:BLOCK] = s[0:BLOCK] + biases[0]
        s_ref[BLOCK:2 * BLOCK] = s[BLOCK:2 * BLOCK]
        s_ref[2 * BLOCK:3 * BLOCK] = s[2 * BLOCK:3 * BLOCK] + biases[1]
        s_ref[3 * BLOCK:] = s[3 * BLOCK:]


def _att_weights(p_ref, s_ref, j, sink_ref, layer):
    sink = jnp.concatenate(
        [jnp.full((1, BLOCK), sink_ref[layer * N_HEADS + j * Q_GROUP + g] * LOG2E, F32)
         for g in range(Q_GROUP)], axis=1)
    m = jnp.maximum(jnp.max(s_ref[...], axis=0, keepdims=True), sink)
    for r in range(0, s_ref.shape[0], BLOCK):
        p_ref[r:r + BLOCK] = jnp.exp2(s_ref[r:r + BLOCK] - m).astype(BF16)
    return jnp.exp2(sink - m)


def _att_values(p_ref, sink_w, j, vt):
    v_ones = jnp.concatenate(
        [vt[j * HEAD_DIM:(j + 1) * HEAD_DIM, :], jnp.ones((16, vt.shape[1]), BF16)], axis=0)
    o = _dot(v_ones, p_ref[...])
    return o[0:HEAD_DIM, :] / (o[HEAD_DIM:HEAD_DIM + 1, :] + sink_w)


N_ATT_STAGES = (TB // BLOCK) * N_KV_HEADS + 2


def _mixer_parts(refs, layer, latent, n, i):
    nb = n // TB
    (pp_ref, pc_ref, pn_ref, gp_ref, gc_ref, gn_ref, q_ref) = refs[:7]
    rest = refs[7:]
    if latent:
        (kp_ref, kc_ref, kn_ref, vp_ref, vc_ref, vn_ref, ck_ref, cv_ref) = rest[:8]
        rest = rest[8:]
    else:
        (ka_ref, va_ref) = rest[:2]
        rest = rest[2:]
    (poolw_ref, pscale_ref, dw_ref, cb_ref, cg_ref, pw_ref, sink_ref,
     o_ref, pext_ref, gext_ref, zp_ref, zg_ref, pooled_ref, act_ref,
     s_ref, p_ref, at_ref) = rest

    operands = []

    def prep():
        _fill_ext(pext_ref, pp_ref, pc_ref, pn_ref, i, nb)
        _shift_copies(pext_ref, zp_ref, (6, 7, 0, 1), slice(0, 128))
        _shift_copies(pext_ref, zp_ref, range(8), slice(128, 256))
        _fill_ext(gext_ref, gp_ref, gc_ref, gn_ref, i, nb)
        _shift_copies(gext_ref, zg_ref, range(8), slice(None))
        if latent:
            k_seq = jnp.concatenate([kp_ref[...], kc_ref[...], kn_ref[...]], axis=0).astype(BF16)
            vt_seq = jnp.concatenate(
                [vp_ref[...], vc_ref[...], vn_ref[...]], axis=0).T.astype(BF16)
            k_ctx, vt_ctx = ck_ref[...].astype(BF16), cv_ref[...].T.astype(BF16)
            cols = Q_GROUP * BLOCK
            key = lax.broadcasted_iota(jnp.int32, (BLOCK, cols), 0)
            r = lax.broadcasted_iota(jnp.int32, (BLOCK, cols), 1) % BLOCK
        else:
            k_all, vt_all = ka_ref[...].astype(BF16), va_ref[...].T.astype(BF16)
        for qb in range(TB // BLOCK):
            q0 = qb * BLOCK
            biases = None
            if latent:
                k_all = jnp.concatenate([k_seq[q0:q0 + 3 * BLOCK], k_ctx], axis=0)
                vt_all = jnp.concatenate([vt_seq[:, q0:q0 + 3 * BLOCK], vt_ctx], axis=1)
                blk = i * (TB // BLOCK) + qb
                biases = (jnp.where((key >= r) & (blk > 0), 0.0, NEG_INF).astype(F32),
                          jnp.where((key <= r) & (blk < n // BLOCK - 1), 0.0, NEG_INF).astype(F32))
            operands.append((q0, k_all, vt_all, biases))

    vpu_chunks = []
    for r0 in range(0, TB, ROW_CHUNK):
        vpu_chunks.append(functools.partial(_pool_chunk, zp_ref, pooled_ref, r0, i, n))
        vpu_chunks.append(functools.partial(_conv_chunk, zg_ref, act_ref, r0, dw_ref, cb_ref, cg_ref))

    def pool_conv_matmuls():
        o_ref[:, 0:POOL_WIDTH] = (
            _dot(pooled_ref[...], poolw_ref[...]) * pscale_ref[...]).astype(BF16)
        o_ref[:, POOL_WIDTH:POOL_WIDTH + CONV_WIDTH] = _dot(act_ref[...], pw_ref[...]).astype(BF16)

    units = [(qb, j) for qb in range(TB // BLOCK) for j in range(N_KV_HEADS)]
    sink_w = {}

    def att_stage(t):
        if t < len(units):
            qb, j = units[t]
            q0, k_all, _, biases = operands[qb]
            _att_scores(s_ref.at[t % 2], q_ref, q0, j, k_all, biases)
        if 0 <= t - 1 < len(units):
            u = t - 1
            sink_w[u] = _att_weights(p_ref.at[u % 2], s_ref.at[u % 2], units[u][1], sink_ref, layer)
        if 0 <= t - 2 < len(units):
            u = t - 2
            qb, j = units[u]
            o = _att_values(p_ref.at[u % 2], sink_w.pop(u), j, operands[qb][2])
            for g in range(Q_GROUP):
                c0 = (j * Q_GROUP + g) * HEAD_DIM
                at_ref[qb, c0:c0 + HEAD_DIM, :] = o[:, g * BLOCK:(g + 1) * BLOCK]

    def att_store():
        for qb in range(TB // BLOCK):
            o_ref[qb * BLOCK:(qb + 1) * BLOCK, POOL_WIDTH + CONV_WIDTH:] = (
                at_ref[qb].T.astype(BF16))

    return prep, att_stage, vpu_chunks, pool_conv_matmuls, att_store


def _mixer_items(parts):
    prep, att_stage, vpu_chunks, pool_conv_matmuls, att_store = parts
    per_stage = -(-len(vpu_chunks) // N_ATT_STAGES)

    def stage(t):
        att_stage(t)
        for chunk in vpu_chunks[t * per_stage:(t + 1) * per_stage]:
            chunk()

    def finish():
        pool_conv_matmuls()
        att_store()

    return [prep] + [functools.partial(stage, t) for t in range(N_ATT_STAGES)] + [finish]


def _mixer_kernel(*refs, layer, latent, n):
    prep, att_stage, vpu_chunks, pool_conv_matmuls, att_store = _mixer_parts(
        refs, layer, latent, n, pl.program_id(1))
    prep()
    if latent:
        for chunk in vpu_chunks:
            chunk()
        pool_conv_matmuls()
        for t in range(N_ATT_STAGES):
            att_stage(t)
    else:
        per_stage = -(-len(vpu_chunks) // N_ATT_STAGES)
        for t in range(N_ATT_STAGES):
            att_stage(t)
            for chunk in vpu_chunks[t * per_stage:(t + 1) * per_stage]:
                chunk()
        pool_conv_matmuls()
    att_store()


def _mixer(pool_in, glu, q, k, v, wts, layer, cache=None):
    b, n, _ = pool_in.shape
    nb = n // TB
    hpb = TB // HALO
    nh = n // HALO
    kpb = TB // BLOCK
    nk = n // BLOCK
    n_keys = n if cache is None else 3 * BLOCK + cache[0].shape[2]

    def cur(width):
        return pl.BlockSpec((None, TB, width), lambda bi, i: (bi, i, 0))

    def halo_prev(width):
        return pl.BlockSpec((None, HALO, width), lambda bi, i: (bi, jnp.maximum(i * hpb - 1, 0), 0))

    def halo_next(width):
        return pl.BlockSpec((None, HALO, width),
                            lambda bi, i: (bi, jnp.minimum((i + 1) * hpb, nh - 1), 0))

    in_specs = [halo_prev(POOL_WIDTH), cur(POOL_WIDTH), halo_next(POOL_WIDTH),
                halo_prev(CONV_WIDTH), cur(CONV_WIDTH), halo_next(CONV_WIDTH),
                cur(ATTN_WIDTH)]
    args = [pool_in, pool_in, pool_in, glu, glu, glu, q]
    if cache is not None:
        cache_k, cache_v = cache
        past = cache_k.shape[2]
        blk_prev = pl.BlockSpec((None, BLOCK, KV_WIDTH),
                                lambda bi, i: (bi, jnp.maximum(i * kpb - 1, 0), 0))
        blk_next = pl.BlockSpec((None, BLOCK, KV_WIDTH),
                                lambda bi, i: (bi, jnp.minimum((i + 1) * kpb, nk - 1), 0))
        cspec = pl.BlockSpec((None, None, past, KV_WIDTH), lambda bi, i: (bi, layer, 0, 0))
        in_specs += [blk_prev, cur(KV_WIDTH), blk_next, blk_prev, cur(KV_WIDTH), blk_next,
                     cspec, cspec]
        args += [k, k, k, v, v, v, cache_k, cache_v]
    else:
        whole = pl.BlockSpec((None, n, KV_WIDTH), lambda bi, i: (bi, 0, 0))
        in_specs += [whole, whole]
        args += [k, v]
    in_specs += _mixer_weight_specs(layer)
    args += _mixer_weight_args(wts)
    return pl.pallas_call(
        functools.partial(_mixer_kernel, layer=layer, latent=cache is not None, n=n),
        grid=(b, nb),
        in_specs=in_specs,
        out_specs=pl.BlockSpec((None, TB, MIX_WIDTH), lambda bi, i: (bi, i, 0)),
        out_shape=jax.ShapeDtypeStruct((b, n, MIX_WIDTH), BF16),
        scratch_shapes=_mixer_scratch(n_keys),
        compiler_params=pltpu.CompilerParams(
            dimension_semantics=("arbitrary", "arbitrary"), vmem_limit_bytes=VMEM_LIMIT),
        name="mixer_latent" if cache is not None else "mixer_context",
    )(*args)


def _mixer_weight_specs(layer):
    def const(shape, index):
        return pl.BlockSpec(shape, lambda *_: index)

    return [
        const((None, POOL_WIDTH, POOL_WIDTH), (layer, 0, 0)),
        const((None, 1, POOL_WIDTH), (layer, 0, 0)),
        const((None, CONV_TAPS, CONV_WIDTH), (layer, 0, 0)),
        const((None, 1, CONV_WIDTH), (layer, 0, 0)),
        const((None, 1, CONV_WIDTH), (layer, 0, 0)),
        const((None, CONV_WIDTH, CONV_WIDTH), (layer, 0, 0)),
        pl.BlockSpec(memory_space=pltpu.SMEM),
    ]


def _mixer_weight_args(wts):
    return [wts["pool_bd"], wts["pool_scale"], wts["conv_dw"], wts["conv_b"],
            wts["conv_norm_g"], wts["conv_pw"], wts["sink"]]


def _mixer_scratch(n_keys):
    return [pltpu.VMEM((TB + 2 * HALO, POOL_WIDTH), F32),
            pltpu.VMEM((TB + 2 * HALO, CONV_WIDTH), F32),
            pltpu.VMEM((8, ZROWS, POOL_WIDTH), F32),
            pltpu.VMEM((8, ZROWS, CONV_WIDTH), F32),
            pltpu.VMEM((TB, POOL_WIDTH), BF16),
            pltpu.VMEM((TB, CONV_WIDTH), BF16),
            pltpu.VMEM((2, n_keys, Q_GROUP * BLOCK), F32),
            pltpu.VMEM((2, n_keys, Q_GROUP * BLOCK), BF16),
            pltpu.VMEM((TB // BLOCK, ATTN_WIDTH, BLOCK), F32)]


N_CTX_BLOCK_OPERANDS = 9


def _ffn_side_kernel(*refs, n_ffn_in, n_mix_in, n_ffn_scratch, layer, n):
    ffn_in = refs[:n_ffn_in]
    mix_in = refs[n_ffn_in:n_ffn_in + n_mix_in]
    o_ref, mix_out = refs[n_ffn_in + n_mix_in:n_ffn_in + n_mix_in + 2]
    scratch = refs[n_ffn_in + n_mix_in + 2:]
    ffn_scratch, mix_scratch = scratch[:n_ffn_scratch], scratch[n_ffn_scratch:]
    side = []
    for s in range(mix_out.shape[0]):
        block = (tuple(r.at[s] for r in mix_in[:N_CTX_BLOCK_OPERANDS])
                 + tuple(mix_in[N_CTX_BLOCK_OPERANDS:]) + (mix_out.at[s],) + tuple(mix_scratch))
        side.append(_mixer_items(_mixer_parts(block, layer, False, n, 0)))
    _ffn_kernel(*ffn_in, o_ref, *ffn_scratch, pre_outproj=False, side=side)


def _ffn_with_context_mixer(x, mods, norm_g, wi, wo, layer, which, tokens_per_cond, first_row,
                            pool_in, glu, q, k, v, wts, casts=()):
    t = x.shape[0]
    tm = TM_FFN
    b, n, _ = pool_in.shape
    per_step = tm // FFN_SUB
    assert n == TB and b == (t // tm) * per_step
    tok = pl.BlockSpec((tm, D), lambda i: (i, 0))
    ffn_specs = [
        tok,
        _mod_spec(layer, which, tm, tokens_per_cond, first_row),
        _const_spec((None, None, 1, D), (layer, which, 0, 0)),
        _const_spec((D, 2 * D_FF), (0, 0)),
        _const_spec((D_FF, D), (0, 0)),
    ]
    ffn_args = [x, mods, norm_g, wi, wo]

    def seqs(rows, width):
        return pl.BlockSpec((per_step, rows, width), lambda i: (i, 0, 0))

    mix_specs = [seqs(HALO, POOL_WIDTH), seqs(TB, POOL_WIDTH), seqs(HALO, POOL_WIDTH),
                 seqs(HALO, CONV_WIDTH), seqs(TB, CONV_WIDTH), seqs(HALO, CONV_WIDTH),
                 seqs(TB, ATTN_WIDTH), seqs(TB, KV_WIDTH), seqs(TB, KV_WIDTH)]
    mix_args = [pool_in, pool_in, pool_in, glu, glu, glu, q, k, v]
    assert len(mix_specs) == N_CTX_BLOCK_OPERANDS
    mix_specs += _mixer_weight_specs(layer)
    mix_args += _mixer_weight_args(wts)
    ffn_scratch = [pltpu.VMEM((tm // FFN_SUB, FFN_SUB, D_FF), BF16),
                   pltpu.VMEM((tm // FFN_SUB, FFN_SUB, D), BF16)]
    c_in, c_out, c_shapes, c_args = _cast_specs(casts, t // tm)
    body = functools.partial(_ffn_side_kernel, n_ffn_in=len(ffn_specs), n_mix_in=len(mix_specs),
                             n_ffn_scratch=len(ffn_scratch), layer=layer, n=n)
    return pl.pallas_call(
        _with_casts(body, len(ffn_specs) + len(mix_specs), 2, len(casts)),
        grid=(t // tm,),
        in_specs=ffn_specs + mix_specs + c_in,
        out_specs=[tok, pl.BlockSpec((per_step, TB, MIX_WIDTH), lambda i: (i, 0, 0))] + c_out,
        out_shape=[jax.ShapeDtypeStruct((t, D), F32),
                   jax.ShapeDtypeStruct((b, n, MIX_WIDTH), BF16)] + c_shapes,
        scratch_shapes=ffn_scratch + _mixer_scratch(n),
        compiler_params=pltpu.CompilerParams(
            dimension_semantics=("arbitrary",), vmem_limit_bytes=VMEM_LIMIT),
        name="ffn_with_context_mixer",
    )(*ffn_args, *mix_args, *c_args)


def _rope_tables(n):
    t = np.arange(n)
    half = HEAD_DIM // 2
    inv = ROPE_THETA ** (-np.arange(0, half, 2, dtype=np.float64) / half)
    zeros = np.zeros((n, half // 2))
    cos, s_up, s_dn = [], [], []
    for pos in (t // GRID_W, t % GRID_W):
        ang = pos[:, None].astype(np.float64) * inv[None, :]
        cos += [np.cos(ang), np.cos(ang)]
        s_up += [-np.sin(ang), zeros]
        s_dn += [zeros, np.sin(ang)]
    return tuple(jnp.asarray(np.concatenate(parts * N_KV_HEADS, axis=1), dtype=F32)
                 for parts in (cos, s_up, s_dn))


def _block_diag_mean(width):
    idx = jnp.arange(width) // HEAD_DIM
    return jnp.where(idx[:, None] == idx[None, :], 1.0 / HEAD_DIM, 0.0).astype(BF16)


def kernel(x_prompt, x_sample, cache_k, cache_v, c, c_ctx, mod_w, mod_b, norm_g, ffn1_wi, ffn1_wo, ffn2_wi, ffn2_wo, w_in, w_out, pool_w, pool_scale, conv_dw, conv_b, conv_norm_g, conv_pw, q_norm_g, k_norm_g, sink):
    batch, seq, _ = x_prompt.shape
    dec_batch, dec_seq, _ = x_sample.shape
    past = cache_k.shape[2]
    assert 1 + dec_batch <= COND_ROWS

    cond = jnp.zeros((COND_ROWS, D), F32).at[0].set(c_ctx).at[1:1 + dec_batch].set(c)
    mods = _modulations(cond, mod_w, mod_b).reshape(DEPTH, COND_ROWS, 1, N_MOD * D)

    norm_g4 = norm_g.reshape(DEPTH, 3, 1, D)
    wi1, wo1 = ffn1_wi[0].astype(BF16), ffn1_wo[0].astype(BF16)
    wi2 = wo2 = w_in_b = w_out_b = None
    eye = jnp.eye(POOL_GROUPS, dtype=F32)
    pool_bd = (eye[None, :, None, :, None] * pool_w[:, :, :, None, :]).reshape(
        DEPTH, POOL_WIDTH, POOL_WIDTH).astype(BF16)
    wts = {
        "pool_bd": pool_bd,
        "pool_scale": pool_scale.reshape(DEPTH, 1, POOL_WIDTH),
        "conv_dw": conv_dw,
        "conv_b": conv_b.reshape(DEPTH, 1, CONV_WIDTH),
        "conv_norm_g": conv_norm_g.reshape(DEPTH, 1, CONV_WIDTH),
        "conv_pw": conv_pw.astype(BF16),
        "sink": sink.reshape(DEPTH * N_HEADS),
    }
    mq, mk = _block_diag_mean(min(ATTN_WIDTH, MXU_TILE)), _block_diag_mean(min(KV_WIDTH, MXU_TILE))
    qg = jnp.tile(q_norm_g, (1, N_HEADS)).reshape(DEPTH, 1, ATTN_WIDTH) * (SCALE * LOG2E)
    kg = jnp.tile(k_norm_g, (1, N_KV_HEADS)).reshape(DEPTH, 1, KV_WIDTH)
    rope = _rope_tables(dec_seq)
    cache_k4 = cache_k.reshape(dec_batch, DEPTH, past, KV_WIDTH)
    cache_v4 = cache_v.reshape(dec_batch, DEPTH, past, KV_WIDTH)

    def seqs(a, bsz, n):
        return a.reshape(bsz, n, a.shape[-1])

    yp = x_prompt.reshape(batch * seq, D)
    ys = x_sample.reshape(dec_batch * dec_seq, D)
    ks, vs = [], []
    for layer in range(DEPTH):
        def nxt(*ws):
            return [(w, layer + 1) for w in ws] if layer + 1 < DEPTH else []

        def first(*ws):
            return [(w, 0) for w in ws] if layer == 0 else []

        yp, *now = _ffn(yp, mods, norm_g4, wi1, wo1, layer, 0, None, 0,
                        casts=first(w_in, w_out, ffn2_wo))
        if layer == 0:
            w_in_b, w_out_b, wo2 = now
        *c_parts, = _proj(yp, mods, norm_g4, w_in_b, mq, mk, qg, kg, layer, None, 0,
                          casts=first(ffn2_wi))
        if layer == 0:
            wi2 = c_parts.pop()
        ys, mix_c, *wo1_n = _ffn_with_context_mixer(
            ys, mods, norm_g4, wi1, wo1, layer, 0, dec_seq, 1,
            *(seqs(a, batch, seq) for a in c_parts), wts, casts=nxt(ffn1_wo))
        l_parts = _proj(ys, mods, norm_g4, w_in_b, mq, mk, qg, kg, layer, dec_seq, 1,
                        rope_tables=rope)
        yp, *wi2_n = _ffn(yp, mods, norm_g4, wi2, wo2, layer, 2, None, 0,
                          mix=mix_c.reshape(batch * seq, MIX_WIDTH), w_out=w_out_b,
                          casts=nxt(ffn2_wi))
        mix_l = _mixer(*(seqs(a, dec_batch, dec_seq) for a in l_parts), wts, layer,
                       cache=(cache_k4, cache_v4))
        ys, *rest_n = _ffn(ys, mods, norm_g4, wi2, wo2, layer, 2, dec_seq, 1,
                           mix=mix_l.reshape(dec_batch * dec_seq, MIX_WIDTH), w_out=w_out_b,
                           casts=nxt(ffn2_wo, w_in, w_out, ffn1_wi))
        ks.append(c_parts[3].reshape(batch, seq, N_KV_HEADS, HEAD_DIM))
        vs.append(c_parts[4].reshape(batch, seq, N_KV_HEADS, HEAD_DIM))
        if layer + 1 < DEPTH:
            (wo1,), (wi2,), (wo2, w_in_b, w_out_b, wi1) = wo1_n, wi2_n, rest_n
    return (yp.reshape(batch, seq, D), ys.reshape(dec_batch, dec_seq, D),
            jnp.stack(ks, axis=1), jnp.stack(vs, axis=1))
```
